```python
import math
import jax, jax.numpy as jnp
from jax import lax
import numpy as np

D_MODEL = 4096
BATCH = 2
SEQ = 4096
DEPTH = 1
DEC_BATCH = 128
DEC_SEQ = 1
PAST_LEN = 2048
PAGE_SIZE = 128

DIFF_HEADS = 8
DIFF_DC = 128
DIFF_DV = 2 * DIFF_DC
DIFF_WIDTH = DIFF_HEADS * DIFF_DV
FOX_HEADS = 16
FOX_DH = 128
FOX_WIDTH = FOX_HEADS * FOX_DH
PEER_HEADS = 8
PEER_NKEYS = 128
PEER_N = PEER_NKEYS * PEER_NKEYS
PEER_DKEY = 256
PEER_DHALF = PEER_DKEY // 2
PEER_TOPK = 16
Q_BLOCK = 128
TOK_BLOCK = 128
ROPE_THETA = 10000.0
EPS = 1e-6
FORGET_BIAS = 2.0
POOL_NUM = 5
POOL_DEN = 4
N_DQ = DIFF_HEADS * 2 * DIFF_DC
N_DK = DIFF_HEADS * 2 * DIFF_DC
N_DV = DIFF_WIDTH
N_FQ = FOX_WIDTH
N_FK = FOX_WIDTH
N_FV = FOX_WIDTH
N_FL = FOX_HEADS
N_GATE = D_MODEL
IN_SIZES = (N_DQ, N_DK, N_DV, N_FQ, N_FK, N_FV, N_FL, N_GATE, N_GATE)
N_IN = N_DQ + N_DK + N_DV + N_FQ + N_FK + N_FV + N_FL + 2 * N_GATE

kernel_name = "diff_fox_gated_peer_decoder_step"


def rmsnorm(x, g):
    xf = x.astype(jnp.float32)
    y = xf * lax.rsqrt(jnp.mean(xf * xf, axis=-1, keepdims=True) + EPS)
    return (y * g.astype(jnp.float32)).astype(x.dtype)


def rope(x, pos):
    half = x.shape[-1] // 2
    inv = ROPE_THETA ** (-jnp.arange(half, dtype=jnp.float32) / half)
    ang = pos.astype(jnp.float32)[:, None] * inv[None, :]
    shape = (pos.shape[0],) + (1,) * (x.ndim - 3) + (half,)
    cos = jnp.cos(ang).reshape(shape)
    sin = jnp.sin(ang).reshape(shape)
    xf = x.astype(jnp.float32)
    x1, x2 = xf[..., :half], xf[..., half:]
    return jnp.concatenate([x1 * cos - x2 * sin, x2 * cos + x1 * sin], axis=-1).astype(x.dtype)


def mixer_project(h, pos, w_in, b_f, dq_g, dk_g, fq_g, fk_g):
    B, S, _ = h.shape
    z = jnp.einsum('bsd,de->bse', h, w_in)
    splits = np.cumsum(np.array(IN_SIZES))[:-1].tolist()
    dq, dk, dv, fq, fk, fv, fl, ga, gb = jnp.split(z, splits, axis=-1)
    dq = rope(rmsnorm(dq.reshape(B, S, DIFF_HEADS, 2, DIFF_DC), dq_g), pos)
    dk = rope(rmsnorm(dk.reshape(B, S, DIFF_HEADS, 2, DIFF_DC), dk_g), pos)
    dv = dv.reshape(B, S, DIFF_HEADS, DIFF_DV)
    fq = rmsnorm(fq.reshape(B, S, FOX_HEADS, FOX_DH), fq_g)
    fk = rmsnorm(fk.reshape(B, S, FOX_HEADS, FOX_DH), fk_g)
    fv = fv.reshape(B, S, FOX_HEADS, FOX_DH)
    logf = jax.nn.log_sigmoid(fl.astype(jnp.float32) + b_f.astype(jnp.float32))
    return dq, dk, dv, fq, fk, fv, logf, ga, gb


def diff_attn(q, k, v, mask, lam):
    s = jnp.einsum('bqhcd,bkhcd->bhcqk', q, k).astype(jnp.float32) * (DIFF_DC ** -0.5)
    p = jax.nn.softmax(jnp.where(mask, s, -jnp.inf), axis=-1)
    a = p[:, :, 0] - lam * p[:, :, 1]
    return jnp.einsum('bhqk,bkhe->bqhe', a.astype(v.dtype), v)


def fox_attn(q, k, v, cq, ck, mask):
    s = jnp.einsum('bqhd,bkhd->bhqk', q, k).astype(jnp.float32) * (FOX_DH ** -0.5)
    bias = jnp.swapaxes(cq, 1, 2)[..., :, None] - jnp.swapaxes(ck, 1, 2)[..., None, :]
    p = jax.nn.softmax(jnp.where(mask, s + bias, -jnp.inf), axis=-1)
    return jnp.einsum('bhqk,bkhd->bqhd', p.astype(v.dtype), v)


def prompt_attention(dq, dk, dv, fq, fk, fv, c, lam):
    B, S = dq.shape[0], dq.shape[1]
    kpos = jnp.arange(S)

    def block(i):
        start = i * Q_BLOCK
        qpos = start + jnp.arange(Q_BLOCK)
        mask = qpos[:, None] >= kpos[None, :]
        dqb = lax.dynamic_slice_in_dim(dq, start, Q_BLOCK, axis=1)
        fqb = lax.dynamic_slice_in_dim(fq, start, Q_BLOCK, axis=1)
        cqb = lax.dynamic_slice_in_dim(c, start, Q_BLOCK, axis=1)
        return diff_attn(dqb, dk, dv, mask, lam), fox_attn(fqb, fk, fv, cqb, c, mask)

    od, of = lax.map(block, jnp.arange(S // Q_BLOCK))
    od = jnp.moveaxis(od, 0, 1).reshape(B, S, DIFF_HEADS, DIFF_DV)
    of = jnp.moveaxis(of, 0, 1).reshape(B, S, FOX_HEADS, FOX_DH)
    return od, of


def merge_out(od, of, ga, gb, lam_init, dsub_g, w_a, w_b, w_o):
    B, S = od.shape[0], od.shape[1]
    od = (rmsnorm(od, dsub_g) * (1.0 - lam_init)).reshape(B, S, DIFF_WIDTH)
    of = of.reshape(B, S, FOX_WIDTH)
    ya = jnp.einsum('bse,ed->bsd', od, w_a)
    yb = jnp.einsum('bse,ed->bsd', of, w_b)
    mix = jax.nn.sigmoid(ga) * ya + jax.nn.sigmoid(gb) * yb
    return jnp.einsum('bsd,de->bse', mix, w_o)


def peer_block(hb, wq, sub_keys, u_tab, v_tab):
    T = hb.shape[0]
    q = jnp.einsum('td,de->te', hb, wq).reshape(T, PEER_HEADS, 2, PEER_DHALF)
    s = jnp.einsum('thcd,hcnd->thcn', q, sub_keys).astype(jnp.float32)
    v_top, i_top = lax.top_k(s, PEER_TOPK)
    cand = v_top[:, :, 0, :, None] + v_top[:, :, 1, None, :]
    cidx = i_top[:, :, 0, :, None] * PEER_NKEYS + i_top[:, :, 1, None, :]
    cand = cand.reshape(T, PEER_HEADS, PEER_TOPK * PEER_TOPK)
    cidx = cidx.reshape(T, PEER_HEADS, PEER_TOPK * PEER_TOPK)
    best, sel = lax.top_k(cand, PEER_TOPK)
    idx = jnp.take_along_axis(cidx, sel, axis=-1)
    g = jax.nn.softmax(best, axis=-1)
    u = jnp.take(u_tab, idx, axis=0)
    a = jnp.einsum('td,thkd->thk', hb, u).astype(jnp.float32)
    w = (g * jax.nn.gelu(a, approximate=False)).astype(hb.dtype)
    vv = jnp.take(v_tab, idx, axis=0)
    return jnp.einsum('thk,thkd->td', w, vv)


def peer_ffn(h, wq, sub_keys, u_tab, v_tab):
    shp = h.shape
    ht = h.reshape(-1, shp[-1])
    n = ht.shape[0]
    nb = -(-n // TOK_BLOCK)
    ht = jnp.pad(ht, ((0, nb * TOK_BLOCK - n), (0, 0)))
    out = lax.map(lambda hb: peer_block(hb, wq, sub_keys, u_tab, v_tab),
                  ht.reshape(nb, TOK_BLOCK, shp[-1]))
    return out.reshape(nb * TOK_BLOCK, shp[-1])[:n].reshape(shp)


def gather_pages(pool, page_table):
    g = jnp.take(pool, page_table, axis=0)
    return g.reshape((g.shape[0], g.shape[1] * g.shape[2]) + g.shape[3:])


def setup_inputs(seed: int = 0) -> dict:
    key = jax.random.key(seed)
    ks = jax.random.split(key, 32)
    nrm = jax.random.normal
    n_pages = PAST_LEN // PAGE_SIZE
    n_used = DEC_BATCH * n_pages
    n_pool = (n_used * POOL_NUM) // POOL_DEN
    page_table = jax.random.permutation(ks[0], n_pool)[:n_used].reshape(DEC_BATCH, n_pages).astype(jnp.int32)

    def gain(k, n):
        return 1.0 + 0.02 * nrm(k, (DEPTH, n))

    return {
        "x_prompt": nrm(ks[1], (BATCH, SEQ, D_MODEL)),
        "x_sample": nrm(ks[2], (DEC_BATCH, DEC_SEQ, D_MODEL)),
        "cache_diff_k": nrm(ks[3], (DEPTH, n_pool, PAGE_SIZE, DIFF_HEADS, 2 * DIFF_DC)),
        "cache_diff_v": nrm(ks[4], (DEPTH, n_pool, PAGE_SIZE, DIFF_HEADS, DIFF_DV)),
        "cache_fox_k": nrm(ks[5], (DEPTH, n_pool, PAGE_SIZE, FOX_HEADS, FOX_DH)),
        "cache_fox_v": nrm(ks[6], (DEPTH, n_pool, PAGE_SIZE, FOX_HEADS, FOX_DH)),
        "cache_fox_logf": jax.nn.log_sigmoid(FORGET_BIAS + nrm(ks[7], (DEPTH, n_pool, PAGE_SIZE, FOX_HEADS))),
        "page_table": page_table,
        "norm1_g": gain(ks[8], D_MODEL),
        "w_in": nrm(ks[9], (DEPTH, D_MODEL, N_IN)) * D_MODEL ** -0.5,
        "b_f": FORGET_BIAS + 0.1 * nrm(ks[10], (DEPTH, FOX_HEADS)),
        "dq_g": gain(ks[11], DIFF_DC),
        "dk_g": gain(ks[12], DIFF_DC),
        "fq_g": gain(ks[13], FOX_DH),
        "fk_g": gain(ks[14], FOX_DH),
        "lam_q1": 0.1 * nrm(ks[15], (DEPTH, DIFF_DC)),
        "lam_k1": 0.1 * nrm(ks[16], (DEPTH, DIFF_DC)),
        "lam_q2": 0.1 * nrm(ks[17], (DEPTH, DIFF_DC)),
        "lam_k2": 0.1 * nrm(ks[18], (DEPTH, DIFF_DC)),
        "dsub_g": gain(ks[19], DIFF_DV),
        "w_a": nrm(ks[20], (DEPTH, DIFF_WIDTH, D_MODEL)) * DIFF_WIDTH ** -0.5,
        "w_b": nrm(ks[21], (DEPTH, FOX_WIDTH, D_MODEL)) * FOX_WIDTH ** -0.5,
        "w_o": nrm(ks[22], (DEPTH, D_MODEL, D_MODEL)) * D_MODEL ** -0.5,
        "norm2_g": gain(ks[23], D_MODEL),
        "peer_wq": nrm(ks[24], (DEPTH, D_MODEL, PEER_HEADS * PEER_DKEY)) * D_MODEL ** -0.5,
        "peer_keys": nrm(ks[25], (DEPTH, PEER_HEADS, 2, PEER_NKEYS, PEER_DHALF)) * PEER_DHALF ** -0.5,
        "peer_u": nrm(ks[26], (DEPTH, PEER_N, D_MODEL)) * D_MODEL ** -0.5,
        "peer_v": nrm(ks[27], (DEPTH, PEER_N, D_MODEL)) * PEER_HEADS ** -0.5,
    }


def reference(x_prompt, x_sample, cache_diff_k, cache_diff_v, cache_fox_k, cache_fox_v, cache_fox_logf,
              page_table, norm1_g, w_in, b_f, dq_g, dk_g, fq_g, fk_g, lam_q1, lam_k1, lam_q2, lam_k2,
              dsub_g, w_a, w_b, w_o, norm2_g, peer_wq, peer_keys, peer_u, peer_v):
    past = page_table.shape[1] * PAGE_SIZE
    B, S = x_prompt.shape[0], x_prompt.shape[1]
    DB, DS = x_sample.shape[0], x_sample.shape[1]
    pos_p = jnp.arange(S, dtype=jnp.int32)
    pos_s = past + jnp.arange(DS, dtype=jnp.int32)
    mask_s = pos_s[:, None] >= jnp.arange(past + DS)[None, :]
    xp, xs = x_prompt, x_sample
    p_dk, p_dv, p_fk, p_fv, p_fl = [], [], [], [], []
    s_dk, s_dv, s_fk, s_fv, s_fl = [], [], [], [], []
    for l in range(DEPTH):
        lam_init = 0.8 - 0.6 * math.exp(-0.3 * l)
        lam = (jnp.exp(jnp.sum(lam_q1[l].astype(jnp.float32) * lam_k1[l].astype(jnp.float32)))
               - jnp.exp(jnp.sum(lam_q2[l].astype(jnp.float32) * lam_k2[l].astype(jnp.float32)))
               + lam_init)

        hp = rmsnorm(xp, norm1_g[l])
        dq, dk, dv, fq, fk, fv, logf, ga, gb = mixer_project(hp, pos_p, w_in[l], b_f[l], dq_g[l], dk_g[l], fq_g[l], fk_g[l])
        c = jnp.cumsum(logf, axis=1)
        od, of = prompt_attention(dq, dk, dv, fq, fk, fv, c, lam)
        xp = xp + merge_out(od, of, ga, gb, lam_init, dsub_g[l], w_a[l], w_b[l], w_o[l])
        xp = xp + peer_ffn(rmsnorm(xp, norm2_g[l]), peer_wq[l], peer_keys[l], peer_u[l], peer_v[l])
        p_dk.append(dk.reshape(B, S, DIFF_HEADS, 2 * DIFF_DC))
        p_dv.append(dv)
        p_fk.append(fk)
        p_fv.append(fv)
        p_fl.append(logf)

        hs = rmsnorm(xs, norm1_g[l])
        sdq, sdk, sdv, sfq, sfk, sfv, slogf, sga, sgb = mixer_project(hs, pos_s, w_in[l], b_f[l], dq_g[l], dk_g[l], fq_g[l], fk_g[l])
        k_d = jnp.concatenate([gather_pages(cache_diff_k[l], page_table).reshape(DB, past, DIFF_HEADS, 2, DIFF_DC).astype(sdk.dtype), sdk], axis=1)
        v_d = jnp.concatenate([gather_pages(cache_diff_v[l], page_table).astype(sdv.dtype), sdv], axis=1)
        k_f = jnp.concatenate([gather_pages(cache_fox_k[l], page_table).astype(sfk.dtype), sfk], axis=1)
        v_f = jnp.concatenate([gather_pages(cache_fox_v[l], page_table).astype(sfv.dtype), sfv], axis=1)
        c_past = jnp.cumsum(gather_pages(cache_fox_logf[l], page_table).astype(jnp.float32), axis=1)
        c_new = c_past[:, -1:] + jnp.cumsum(slogf, axis=1)
        ck = jnp.concatenate([c_past, c_new], axis=1)
        sod = diff_attn(sdq, k_d, v_d, mask_s, lam)
        sof = fox_attn(sfq, k_f, v_f, c_new, ck, mask_s)
        xs = xs + merge_out(sod, sof, sga, sgb, lam_init, dsub_g[l], w_a[l], w_b[l], w_o[l])
        xs = xs + peer_ffn(rmsnorm(xs, norm2_g[l]), peer_wq[l], peer_keys[l], peer_u[l], peer_v[l])
        s_dk.append(sdk.reshape(DB, DS, DIFF_HEADS, 2 * DIFF_DC))
        s_dv.append(sdv)
        s_fk.append(sfk)
        s_fv.append(sfv)
        s_fl.append(slogf)

    return (xp, xs,
            jnp.stack(p_dk), jnp.stack(p_dv), jnp.stack(p_fk), jnp.stack(p_fv), jnp.stack(p_fl),
            jnp.stack(s_dk), jnp.stack(s_dv), jnp.stack(s_fk), jnp.stack(s_fv), jnp.stack(s_fl))
```

```python
import functools
import math

import jax
import jax.numpy as jnp
from jax import lax
from jax.experimental import pallas as pl
from jax.experimental.pallas import tpu as pltpu

DIFF_HEADS = 8
DIFF_DC = 128
DIFF_DV = 2 * DIFF_DC
FOX_HEADS = 16
FOX_DH = 128
PEER_HEADS = 8
PEER_NKEYS = 128
PEER_DHALF = 128
PEER_TOPK = 16
PAGE_SIZE = 128
ROPE_THETA = 10000.0
EPS = 1e-6
LANES = 128
VMEM_LIMIT = 56 * 1024 * 1024

F32 = jnp.float32
BF16 = jnp.bfloat16
NEG_INF = float("-inf")


def _cparams(*sem):
    return pltpu.CompilerParams(dimension_semantics=sem, vmem_limit_bytes=VMEM_LIMIT)


def _tile(n, pref):
    t = min(n, pref)
    while n % t:
        t //= 2
    return t


def _split3(x):
    hi = x.astype(BF16)
    r1 = x - hi.astype(F32)
    mid = r1.astype(BF16)
    lo = (r1 - mid.astype(F32)).astype(BF16)
    return hi, mid, lo


def _log_sigmoid(x):
    return jnp.minimum(x, 0.0) - jnp.log1p(jnp.exp(-jnp.abs(x)))


def _sigmoid(x):
    return 1.0 / (1.0 + jnp.exp(-x))


def _rmsnorm_kernel(x_ref, g_ref, o_ref):
    x = x_ref[...]
    ms = jnp.mean(x * x, axis=-1, keepdims=True)
    o_ref[...] = (x * lax.rsqrt(ms + EPS) * g_ref[...]).astype(o_ref.dtype)


def rmsnorm_cast(x, g):
    T, D = x.shape
    tm = _tile(T, 256)
    return pl.pallas_call(
        _rmsnorm_kernel,
        grid=(T // tm,),
        in_specs=[pl.BlockSpec((tm, D), lambda i: (i, 0)), pl.BlockSpec((1, D), lambda i: (0, 0))],
        out_specs=pl.BlockSpec((tm, D), lambda i: (i, 0)),
        out_shape=jax.ShapeDtypeStruct((T, D), BF16),
        compiler_params=_cparams("parallel"),
        name="rmsnorm_cast",
    )(x, g.reshape(1, D))


def _proj_kernel(*refs, norm, rope, act, scale):
    h_ref, w_ref = refs[0], refs[1]
    pos = 2
    gain_ref = cos_ref = sin_ref = None
    if norm:
        gain_ref = refs[pos]
        pos += 1
    if rope:
        cos_ref, sin_ref = refs[pos], refs[pos + 1]
        pos += 2
    out_refs = refs[pos:]
    z = jnp.dot(h_ref[...], w_ref[...], preferred_element_type=F32)
    tn = z.shape[1]
    if norm:
        for g in range(tn // LANES):
            x = z[:, g * LANES:(g + 1) * LANES]
            ms = jnp.mean(x * x, axis=-1, keepdims=True)
            y = x * lax.rsqrt(ms + EPS) * gain_ref[...]
            if rope:
                y = y * cos_ref[...] + pltpu.roll(y, LANES // 2, 1) * sin_ref[...]
            if scale != 1.0:
                y = y * scale
            for o_ref in out_refs:
                o_ref[:, g * LANES:(g + 1) * LANES] = y.astype(o_ref.dtype)
    else:
        if act == "sigmoid":
            z = _sigmoid(z)
        for o_ref in out_refs:
            o_ref[...] = z.astype(o_ref.dtype)


def project(h, w, *, out_dtypes, gain=None, cos=None, sin=None, act=None, scale=1.0, tm_pref=1024, tn_pref=512):
    T, D = h.shape
    N = w.shape[1]
    tm = _tile(T, tm_pref)
    tn = _tile(N, tn_pref)
    norm = gain is not None
    rope = cos is not None
    in_specs = [pl.BlockSpec((tm, D), lambda i, j: (i, 0)), pl.BlockSpec((D, tn), lambda i, j: (0, j))]
    args = [h, w]
    if norm:
        in_specs.append(pl.BlockSpec((1, LANES), lambda i, j: (0, 0)))
        args.append(gain.reshape(1, LANES))
    if rope:
        nrep = cos.shape[0] // tm
        in_specs += [pl.BlockSpec((tm, LANES), lambda i, j: (i % nrep, 0))] * 2
        args += [cos, sin]
    outs = pl.pallas_call(
        functools.partial(_proj_kernel, norm=norm, rope=rope, act=act, scale=scale),
        grid=(T // tm, N // tn),
        in_specs=in_specs,
        out_specs=[pl.BlockSpec((tm, tn), lambda i, j: (i, j)) for _ in out_dtypes],
        out_shape=[jax.ShapeDtypeStruct((T, N), dt) for dt in out_dtypes],
        compiler_params=_cparams("parallel", "arbitrary"),
        name="project",
    )(*args)
    return outs


def _forget_kernel(h_ref, w_ref, wt_ref, b_ref, bt_ref, logf_ref, c_ref, ct_ref, carry, carry_t):
    s = pl.program_id(1)

    @pl.when(s == 0)
    def _():
        carry[...] = jnp.zeros_like(carry)
        carry_t[...] = jnp.zeros_like(carry_t)

    h = h_ref[...]
    ts = h.shape[0]
    fl = jnp.dot(h, w_ref[...], preferred_element_type=F32)
    logf = _log_sigmoid(fl + b_ref[...])
    logf_ref[...] = logf
    row = lax.broadcasted_iota(jnp.int32, (ts, ts), 0)
    col = lax.broadcasted_iota(jnp.int32, (ts, ts), 1)
    lower = (row >= col).astype(BF16)
    cum = sum(jnp.dot(lower, p, preferred_element_type=F32) for p in _split3(logf))
    c = cum + carry[...]
    c_ref[...] = c
    carry[...] = c[ts - 1:ts, :]
    flt = lax.dot_general(wt_ref[...], h, (((1,), (1,)), ((), ())), preferred_element_type=F32)
    logft = _log_sigmoid(flt + bt_ref[...])
    upper = (row <= col).astype(BF16)
    cumt = sum(jnp.dot(p, upper, preferred_element_type=F32) for p in _split3(logft))
    ct = cumt + carry_t[...]
    ct_ref[...] = ct
    carry_t[...] = ct[:, ts - 1:ts]


def forget_gate(h, w_fl, b_f, B, S):
    D = h.shape[1]
    H = w_fl.shape[1]
    ts = _tile(S, 512)
    ns = S // ts
    return pl.pallas_call(
        _forget_kernel,
        grid=(B, ns),
        in_specs=[
            pl.BlockSpec((ts, D), lambda b, s: (b * ns + s, 0)),
            pl.BlockSpec((D, H), lambda b, s: (0, 0)),
            pl.BlockSpec((H, D), lambda b, s: (0, 0)),
            pl.BlockSpec((1, H), lambda b, s: (0, 0)),
            pl.BlockSpec((H, 1), lambda b, s: (0, 0)),
        ],
        out_specs=[
            pl.BlockSpec((None, ts, H), lambda b, s: (b, s, 0)),
            pl.BlockSpec((None, ts, H), lambda b, s: (b, s, 0)),
            pl.BlockSpec((None, H, ts), lambda b, s: (b, 0, s)),
        ],
        out_shape=[
            jax.ShapeDtypeStruct((B, S, H), F32),
            jax.ShapeDtypeStruct((B, S, H), F32),
            jax.ShapeDtypeStruct((B, H, S), F32),
        ],
        scratch_shapes=[pltpu.VMEM((1, H), F32), pltpu.VMEM((H, 1), F32)],
        compiler_params=_cparams("parallel", "arbitrary"),
        name="forget_gate",
    )(h, w_fl, w_fl.T, b_f.reshape(1, H), b_f.reshape(H, 1))


def _lambda_value(lq1, lk1, lq2, lk2, lam_init):
    return (jnp.exp(jnp.sum(lq1 * lk1, axis=-1, keepdims=True))
            - jnp.exp(jnp.sum(lq2 * lk2, axis=-1, keepdims=True)) + lam_init)


def _diff_attn_kernel(q_ref, k_ref, v_ref, g_ref, lq1_ref, lk1_ref, lq2_ref, lk2_ref, o_ref,
                      m_scr, l_scr, acc_scr, *, lam_init):
    qi = pl.program_id(2)
    ki = pl.program_id(3)

    @pl.when(ki == 0)
    def _():
        m_scr[...] = jnp.full_like(m_scr, NEG_INF)
        l_scr[...] = jnp.zeros_like(l_scr)
        acc_scr[...] = jnp.zeros_like(acc_scr)

    def step(masked):
        q = q_ref[...]
        k = k_ref[...]
        v = v_ref[...]
        for c in range(2):
            s = lax.dot_general(q[:, c * DIFF_DC:(c + 1) * DIFF_DC], k[:, c * DIFF_DC:(c + 1) * DIFF_DC],
                                (((1,), (1,)), ((), ())), preferred_element_type=F32)
            if masked:
                row = lax.broadcasted_iota(jnp.int32, s.shape, 0)
                col = lax.broadcasted_iota(jnp.int32, s.shape, 1)
                s = jnp.where(row >= col, s, NEG_INF)
            m_prev = m_scr[c]
            m_new = jnp.maximum(m_prev, jnp.max(s, axis=-1, keepdims=True))
            alpha = jnp.exp(m_prev - m_new)
            p = jnp.exp(s - m_new)
            l_scr[c] = alpha * l_scr[c] + jnp.sum(p, axis=-1, keepdims=True)
            acc_scr[c] = alpha * acc_scr[c] + jnp.dot(p.astype(BF16), v, preferred_element_type=F32)
            m_scr[c] = m_new

    @pl.when(ki < qi)
    def _():
        step(False)

    @pl.when(ki == qi)
    def _():
        step(True)
        lam = _lambda_value(lq1_ref[...], lk1_ref[...], lq2_ref[...], lk2_ref[...], lam_init)
        o = acc_scr[0] / l_scr[0] - lam * (acc_scr[1] / l_scr[1])
        ms = jnp.mean(o * o, axis=-1, keepdims=True)
        o_ref[...] = (o * lax.rsqrt(ms + EPS) * g_ref[...] * (1.0 - lam_init)).astype(o_ref.dtype)


def diff_attention(q, k, v, dsub_g, lam_vecs, lam_init, B, S):
    t = _tile(S, 512)
    n = S // t
    W = DIFF_DV
    qspec = pl.BlockSpec((t, W), lambda b, h, qi, ki: (b * n + qi, h))
    kspec = pl.BlockSpec((t, W), lambda b, h, qi, ki: (b * n + jnp.minimum(ki, qi), h))
    vec = pl.BlockSpec((1, DIFF_DC), lambda b, h, qi, ki: (0, 0))
    return pl.pallas_call(
        functools.partial(_diff_attn_kernel, lam_init=lam_init),
        grid=(B, DIFF_HEADS, n, n),
        in_specs=[qspec, kspec, kspec, pl.BlockSpec((1, W), lambda b, h, qi, ki: (0, 0)), vec, vec, vec, vec],
        out_specs=qspec,
        out_shape=jax.ShapeDtypeStruct((B * S, DIFF_HEADS * W), BF16),
        scratch_shapes=[pltpu.VMEM((2, t, 1), F32), pltpu.VMEM((2, t, 1), F32), pltpu.VMEM((2, t, W), F32)],
        compiler_params=_cparams("parallel", "parallel", "parallel", "arbitrary"),
        name="diff_attention",
    )(q, k, v, dsub_g.reshape(1, W), *[x.reshape(1, DIFF_DC) for x in lam_vecs])


def _fox_attn_kernel(q_ref, k_ref, v_ref, c_ref, ct_ref, o_ref, m_scr, l_scr, acc_scr, cq_scr):
    h = pl.program_id(1)
    qi = pl.program_id(2)
    ki = pl.program_id(3)

    @pl.when(ki == 0)
    def _():
        m_scr[...] = jnp.full_like(m_scr, NEG_INF)
        l_scr[...] = jnp.zeros_like(l_scr)
        acc_scr[...] = jnp.zeros_like(acc_scr)
        c = c_ref[...]
        lane = lax.broadcasted_iota(jnp.int32, c.shape, 1)
        cq_scr[...] = jnp.sum(jnp.where(lane == h, c, 0.0), axis=-1, keepdims=True)

    def step(masked):
        s = lax.dot_general(q_ref[...], k_ref[...], (((1,), (1,)), ((), ())), preferred_element_type=F32)
        s = s + (cq_scr[...] - ct_ref[pl.ds(h, 1), :])
        if masked:
            row = lax.broadcasted_iota(jnp.int32, s.shape, 0)
            col = lax.broadcasted_iota(jnp.int32, s.shape, 1)
            s = jnp.where(row >= col, s, NEG_INF)
        m_prev = m_scr[...]
        m_new = jnp.maximum(m_prev, jnp.max(s, axis=-1, keepdims=True))
        alpha = jnp.exp(m_prev - m_new)
        p = jnp.exp(s - m_new)
        l_scr[...] = alpha * l_scr[...] + jnp.sum(p, axis=-1, keepdims=True)
        acc_scr[...] = alpha * acc_scr[...] + jnp.dot(p.astype(BF16), v_ref[...], preferred_element_type=F32)
        m_scr[...] = m_new

    @pl.when(ki < qi)
    def _():
        step(False)

    @pl.when(ki == qi)
    def _():
        step(True)
        o_ref[...] = (acc_scr[...] / l_scr[...]).astype(o_ref.dtype)


def fox_attention(q, k, v, c, ct, B, S):
    t = _tile(S, 512)
    n = S // t
    H = FOX_HEADS
    qspec = pl.BlockSpec((t, FOX_DH), lambda b, h, qi, ki: (b * n + qi, h))
    kspec = pl.BlockSpec((t, FOX_DH), lambda b, h, qi, ki: (b * n + jnp.minimum(ki, qi), h))
    return pl.pallas_call(
        _fox_attn_kernel,
        grid=(B, H, n, n),
        in_specs=[
            qspec, kspec, kspec,
            pl.BlockSpec((None, t, H), lambda b, h, qi, ki: (b, qi, 0)),
            pl.BlockSpec((None, H, t), lambda b, h, qi, ki: (b, 0, jnp.minimum(ki, qi))),
        ],
        out_specs=qspec,
        out_shape=jax.ShapeDtypeStruct((B * S, H * FOX_DH), BF16),
        scratch_shapes=[pltpu.VMEM((t, 1), F32), pltpu.VMEM((t, 1), F32), pltpu.VMEM((t, FOX_DH), F32),
                        pltpu.VMEM((t, 1), F32)],
        compiler_params=_cparams("parallel", "parallel", "parallel", "arbitrary"),
        name="fox_attention",
    )(q, k, v, c, ct)


def _decode_kernel(pt_ref, qd_ref, kd_ref, vd_ref, qf_ref, kf_ref, vf_ref, slogf_ref,
                   pdk_ref, pdv_ref, pfk_ref, pfv_ref, plf_ref,
                   g_ref, lq1_ref, lk1_ref, lq2_ref, lk2_ref,
                   od_ref, of_ref,
                   qd_rows, qf_rows, md, ld, accd, mf, lf, accf, carry, *, lam_init):
    del pt_ref
    p = pl.program_id(1)
    G = qd_rows.shape[0]
    W = qd_rows.shape[1]
    row = lax.broadcasted_iota(jnp.int32, (G, W), 0)
    colgrp = lax.broadcasted_iota(jnp.int32, (G, W), 1) // LANES
    own = row == colgrp

    @pl.when(p == 0)
    def _():
        for q_ref, k_ref, v_ref, q_rows, m, l, acc in ((qd_ref, kd_ref, vd_ref, qd_rows, md, ld, accd),
                                                       (qf_ref, kf_ref, vf_ref, qf_rows, mf, lf, accf)):
            qr = jnp.where(own, q_ref[...].astype(F32), 0.0)
            q_rows[...] = qr.astype(BF16)
            m[...] = jnp.sum(qr * k_ref[...].astype(F32), axis=-1, keepdims=True)
            l[...] = jnp.ones_like(l)
            acc[...] = jnp.broadcast_to(v_ref[...].astype(F32), (G, W))
        carry[...] = jnp.zeros_like(carry)

    def update(s, v_page, m, l, acc):
        m_prev = m[...]
        m_new = jnp.maximum(m_prev, jnp.max(s, axis=-1, keepdims=True))
        alpha = jnp.exp(m_prev - m_new)
        pr = jnp.exp(s - m_new)
        l[...] = alpha * l[...] + jnp.sum(pr, axis=-1, keepdims=True)
        acc[...] = alpha * acc[...] + jnp.dot(pr.astype(BF16), v_page, preferred_element_type=F32)
        m[...] = m_new

    nt = (((1,), (1,)), ((), ()))
    sd = lax.dot_general(qd_rows[...], pdk_ref[...].astype(BF16), nt, preferred_element_type=F32)
    update(sd, pdv_ref[...].astype(BF16), md, ld, accd)

    lft = plf_ref[...]
    r = lax.broadcasted_iota(jnp.int32, (PAGE_SIZE, PAGE_SIZE), 0)
    cidx = lax.broadcasted_iota(jnp.int32, (PAGE_SIZE, PAGE_SIZE), 1)
    after = (r > cidx).astype(BF16)
    suffix = sum(jnp.dot(piece, after, preferred_element_type=F32) for piece in _split3(lft))
    bias = slogf_ref[...] + carry[...] + suffix
    carry[...] = carry[...] + jnp.sum(lft, axis=-1, keepdims=True)
    sf = lax.dot_general(qf_rows[...], pfk_ref[...].astype(BF16), nt, preferred_element_type=F32) + bias
    update(sf, pfv_ref[...].astype(BF16), mf, lf, accf)

    @pl.when(p == pl.num_programs(1) - 1)
    def _():
        lam = _lambda_value(lq1_ref[...], lk1_ref[...], lq2_ref[...], lk2_ref[...], lam_init)
        rr = lax.broadcasted_iota(jnp.int32, (G, W), 0)
        cc = lax.broadcasted_iota(jnp.int32, (G, W), 1)
        in_head = (rr // 2) == (cc // DIFF_DV)
        wrow = jnp.where(rr % 2 == 0, 1.0, -lam) / ld[...]
        o = jnp.sum(jnp.where(in_head, accd[...] * wrow, 0.0), axis=0, keepdims=True)
        for hd in range(W // DIFF_DV):
            oh = o[:, hd * DIFF_DV:(hd + 1) * DIFF_DV]
            ms = jnp.mean(oh * oh, axis=-1, keepdims=True)
            od_ref[:, hd * DIFF_DV:(hd + 1) * DIFF_DV] = (
                oh * lax.rsqrt(ms + EPS) * g_ref[...] * (1.0 - lam_init)).astype(od_ref.dtype)
        of = jnp.sum(jnp.where(own, accf[...] / lf[...], 0.0), axis=0, keepdims=True)
        of_ref[...] = of.astype(of_ref.dtype)


def decode_attention(page_table, qd, kd, vd, qf, kf, vf, slogf_t, cache_dk, cache_dv, cache_fk, cache_fv,
                     cache_lf_t, dsub_g, lam_vecs, lam_init):
    DB, n_pages = page_table.shape
    W = qd.shape[-1]
    H = FOX_HEADS
    G = 16
    assert W == G * LANES and DIFF_HEADS * 2 == G and FOX_HEADS == G
    tok = pl.BlockSpec((None, 1, W), lambda b, p, pt: (b, 0, 0))
    page = pl.BlockSpec((None, PAGE_SIZE, W), lambda b, p, pt: (pt[b, n_pages - 1 - p], 0, 0))
    vec = pl.BlockSpec((1, DIFF_DC), lambda b, p, pt: (0, 0))
    grid_spec = pltpu.PrefetchScalarGridSpec(
        num_scalar_prefetch=1,
        grid=(DB, n_pages),
        in_specs=[
            tok, tok, tok, tok, tok, tok,
            pl.BlockSpec((None, H, 1), lambda b, p, pt: (b, 0, 0)),
            page, page, page, page,
            pl.BlockSpec((None, H, PAGE_SIZE), lambda b, p, pt: (pt[b, n_pages - 1 - p], 0, 0)),
            pl.BlockSpec((1, DIFF_DV), lambda b, p, pt: (0, 0)),
            vec, vec, vec, vec,
        ],
        out_specs=[tok, tok],
        scratch_shapes=[
            pltpu.VMEM((G, W), BF16), pltpu.VMEM((G, W), BF16),
            pltpu.VMEM((G, 1), F32), pltpu.VMEM((G, 1), F32), pltpu.VMEM((G, W), F32),
            pltpu.VMEM((G, 1), F32), pltpu.VMEM((G, 1), F32), pltpu.VMEM((G, W), F32),
            pltpu.VMEM((H, 1), F32),
        ],
    )
    return pl.pallas_call(
        functools.partial(_decode_kernel, lam_init=lam_init),
        grid_spec=grid_spec,
        out_shape=[jax.ShapeDtypeStruct((DB, 1, W), BF16), jax.ShapeDtypeStruct((DB, 1, W), BF16)],
        compiler_params=_cparams("parallel", "arbitrary"),
        name="decode_attention",
    )(page_table, qd, kd, vd, qf, kf, vf, slogf_t, cache_dk, cache_dv, cache_fk, cache_fv, cache_lf_t,
      dsub_g.reshape(1, DIFF_DV), *[x.reshape(1, DIFF_DC) for x in lam_vecs])


def _merge_kernel(od_ref, of_ref, wa_ref, wb_ref, ga_ref, gb_ref, o_ref):
    ya = jnp.dot(od_ref[...], wa_ref[...], preferred_element_type=F32)
    yb = jnp.dot(of_ref[...], wb_ref[...], preferred_element_type=F32)
    o_ref[...] = (ga_ref[...].astype(F32) * ya + gb_ref[...].astype(F32) * yb).astype(o_ref.dtype)


def merge(od, of, w_a, w_b, sga, sgb):
    T, Wd = od.shape
    Wf = of.shape[1]
    D = w_a.shape[1]
    tm = _tile(T, 1024)
    tn = _tile(D, 512)
    return pl.pallas_call(
        _merge_kernel,
        grid=(T // tm, D // tn),
        in_specs=[
            pl.BlockSpec((tm, Wd), lambda i, j: (i, 0)), pl.BlockSpec((tm, Wf), lambda i, j: (i, 0)),
            pl.BlockSpec((Wd, tn), lambda i, j: (0, j)), pl.BlockSpec((Wf, tn), lambda i, j: (0, j)),
            pl.BlockSpec((tm, tn), lambda i, j: (i, j)), pl.BlockSpec((tm, tn), lambda i, j: (i, j)),
        ],
        out_specs=pl.BlockSpec((tm, tn), lambda i, j: (i, j)),
        out_shape=jax.ShapeDtypeStruct((T, D), BF16),
        compiler_params=_cparams("parallel", "arbitrary"),
        name="merge",
    )(od, of, w_a, w_b, sga, sgb)


def _out_proj_kernel(mix_ref, w_ref, x_ref, o_ref):
    o_ref[...] = x_ref[...] + jnp.dot(mix_ref[...], w_ref[...], preferred_element_type=F32)


def out_proj_residual(mix, w_o, x):
    T, D = mix.shape
    N = w_o.shape[1]
    tm = _tile(T, 1024)
    tn = _tile(N, 512)
    return pl.pallas_call(
        _out_proj_kernel,
        grid=(T // tm, N // tn),
        in_specs=[
            pl.BlockSpec((tm, D), lambda i, j: (i, 0)), pl.BlockSpec((D, tn), lambda i, j: (0, j)),
            pl.BlockSpec((tm, tn), lambda i, j: (i, j)),
        ],
        out_specs=pl.BlockSpec((tm, tn), lambda i, j: (i, j)),
        out_shape=jax.ShapeDtypeStruct((T, N), F32),
        compiler_params=_cparams("parallel", "arbitrary"),
        name="out_proj_residual",
    )(mix, w_o, x)


def _peer_query_kernel(x_ref, g_ref, wq_ref, keys_ref, hb_ref, s_ref, hb_scr):
    j = pl.program_id(1)

    @pl.when(j == 0)
    def _():
        x = x_ref[...]
        ms = jnp.mean(x * x, axis=-1, keepdims=True)
        hb = (x * lax.rsqrt(ms + EPS) * g_ref[...]).astype(BF16)
        hb_scr[...] = hb
        hb_ref[...] = hb

    q = jnp.dot(hb_scr[...], wq_ref[...], preferred_element_type=F32).astype(BF16)
    for g in range(q.shape[1] // PEER_DHALF):
        s_ref[:, g * PEER_NKEYS:(g + 1) * PEER_NKEYS] = lax.dot_general(
            q[:, g * PEER_DHALF:(g + 1) * PEER_DHALF], keys_ref[g], (((1,), (1,)), ((), ())),
            preferred_element_type=F32)


def peer_query(x, g, wq, keys):
    T, D = x.shape
    NQ = wq.shape[1]
    tm = _tile(T, 512)
    gpt = 4
    tn = gpt * PEER_DHALF
    return pl.pallas_call(
        _peer_query_kernel,
        grid=(T // tm, NQ // tn),
        in_specs=[
            pl.BlockSpec((tm, D), lambda i, j: (i, 0)), pl.BlockSpec((1, D), lambda i, j: (0, 0)),
            pl.BlockSpec((D, tn), lambda i, j: (0, j)),
            pl.BlockSpec((gpt, PEER_NKEYS, PEER_DHALF), lambda i, j: (j, 0, 0)),
        ],
        out_specs=[pl.BlockSpec((tm, D), lambda i, j: (i, 0)), pl.BlockSpec((tm, gpt * PEER_NKEYS), lambda i, j: (i, j))],
        out_shape=[jax.ShapeDtypeStruct((T, D), BF16),
                   jax.ShapeDtypeStruct((T, (NQ // PEER_DHALF) * PEER_NKEYS), F32)],
        scratch_shapes=[pltpu.VMEM((tm, D), BF16)],
        compiler_params=_cparams("parallel", "arbitrary"),
        name="peer_query",
    )(x, g.reshape(1, D), wq, keys)


def _topk_rank(x, lane, n):
    width = float(x.shape[-1])
    rank = jnp.full(x.shape, float(n), F32)
    vals = []
    for r in range(n):
        m = jnp.max(x, axis=-1, keepdims=True)
        idx = jnp.min(jnp.where(x == m, lane, width), axis=-1, keepdims=True)
        hit = lane == idx
        rank = jnp.where(hit, float(r), rank)
        x = jnp.where(hit, NEG_INF, x)
        vals.append(m)
    return rank, vals


def _peer_topk_kernel(s_ref, n0_ref, r1_ref, e0_ref, e1_ref):
    K = PEER_TOPK
    NK = PEER_NKEYS
    tm = s_ref.shape[0]
    lane = lax.broadcasted_iota(jnp.int32, (tm, NK), 1).astype(F32)
    lane2i = lax.broadcasted_iota(jnp.int32, (tm, K * K), 1)
    lane2 = lane2i.astype(F32)
    first = [lane2i // K == r for r in range(K)]
    second = [lane2i % K == r for r in range(K)]
    for h in range(s_ref.shape[1] // (2 * NK)):
        s0 = s_ref[:, (2 * h) * NK:(2 * h + 1) * NK]
        s1 = s_ref[:, (2 * h + 1) * NK:(2 * h + 2) * NK]
        rank0, v0 = _topk_rank(s0, lane, K)
        rank1, v1 = _topk_rank(s1, lane, K)
        c0 = jnp.zeros((tm, K * K), F32)
        c1 = jnp.zeros((tm, K * K), F32)
        for r in range(K):
            c0 = jnp.where(first[r], v0[r], c0)
            c1 = jnp.where(second[r], v1[r], c1)
        cand = c0 + c1
        crank, cvals = _topk_rank(cand, lane2, K)
        sel = crank < float(K)
        z = jnp.sum(jnp.where(sel, jnp.exp(cand - cvals[0]), 0.0), axis=-1, keepdims=True)
        n0 = jnp.zeros((tm, NK), F32)
        for a in range(K):
            cnt = jnp.sum(jnp.where(sel & first[a], 1.0, 0.0), axis=-1, keepdims=True)
            n0 = jnp.where(rank0 == float(a), cnt, n0)
        n0_ref[:, h * NK:(h + 1) * NK] = n0
        r1_ref[:, h * NK:(h + 1) * NK] = rank1
        e0_ref[:, h * NK:(h + 1) * NK] = jnp.exp(s0 - v0[0])
        e1_ref[:, h * NK:(h + 1) * NK] = jnp.exp(s1 - v1[0]) / z


def peer_topk(scores):
    T, NS = scores.shape
    tm = _tile(T, 128)
    W = NS // 2
    spec = pl.BlockSpec((tm, W), lambda i: (i, 0))
    return pl.pallas_call(
        _peer_topk_kernel,
        grid=(T // tm,),
        in_specs=[pl.BlockSpec((tm, NS), lambda i: (i, 0))],
        out_specs=[spec] * 4,
        out_shape=[jax.ShapeDtypeStruct((T, W), F32)] * 4,
        compiler_params=_cparams("parallel"),
        name="peer_topk",
    )(scores)


def _peer_mix_kernel(hb_ref, u_ref, v_ref, n0_ref, e0_ref, r1_ref, e1_ref, x_ref, o_ref, *, heads):
    e = pl.program_id(1)

    @pl.when(e == 0)
    def _():
        o_ref[...] = x_ref[...]

    NK = PEER_NKEYS
    ipc = u_ref.shape[0] // NK
    a = lax.dot_general(hb_ref[...], u_ref[...], (((1,), (1,)), ((), ())), preferred_element_type=F32)
    act = 0.5 * a * (1.0 + lax.erf(a * (2.0 ** -0.5)))
    n0 = n0_ref[...]
    e0 = e0_ref[...]
    parts = []
    for ii in range(ipc):
        gate = None
        for h in range(heads):
            col = h * ipc + ii
            sel = r1_ref[:, h * NK:(h + 1) * NK] < n0[:, col:col + 1]
            term = jnp.where(sel, e1_ref[:, h * NK:(h + 1) * NK], 0.0) * e0[:, col:col + 1]
            gate = term if gate is None else gate + term
        parts.append((gate * act[:, ii * NK:(ii + 1) * NK]).astype(BF16))
    w = jnp.concatenate(parts, axis=1)
    o_ref[...] += jnp.dot(w, v_ref[...], preferred_element_type=F32)


def peer_mix(hb, u_tab, v_tab, n0, r1, e0, e1, x):
    T, D = hb.shape
    NE = u_tab.shape[0]
    NK = PEER_NKEYS
    heads = n0.shape[1] // NK
    tm = _tile(T, 256)
    ec = 512
    ipc = ec // NK
    nchunk = NE // ec

    def per_chunk(t):
        return t.reshape(T, heads, nchunk, ipc).transpose(2, 0, 1, 3).reshape(nchunk, T, heads * ipc)

    tok = pl.BlockSpec((tm, D), lambda i, e: (i, 0))
    tab = pl.BlockSpec((ec, D), lambda i, e: (e, 0))
    chunk = pl.BlockSpec((None, tm, heads * ipc), lambda i, e: (e, i, 0))
    full = pl.BlockSpec((tm, heads * NK), lambda i, e: (i, 0))
    return pl.pallas_call(
        functools.partial(_peer_mix_kernel, heads=heads),
        grid=(T // tm, nchunk),
        in_specs=[tok, tab, tab, chunk, chunk, full, full, tok],
        out_specs=tok,
        out_shape=jax.ShapeDtypeStruct((T, D), F32),
        compiler_params=_cparams("parallel", "arbitrary"),
        name="peer_mix",
    )(hb, u_tab, v_tab, per_chunk(n0), per_chunk(e0), r1, e1, x)


def _rope_tables(pos, n_rows):
    half = DIFF_DC // 2
    inv = ROPE_THETA ** (-jnp.arange(half, dtype=F32) / half)
    ang = pos.astype(F32)[:, None] * inv[None, :]
    cos = jnp.cos(ang)
    sin = jnp.sin(ang)
    cos2 = jnp.concatenate([cos, cos], axis=-1)
    sin2 = jnp.concatenate([-sin, sin], axis=-1)
    if cos2.shape[0] != n_rows:
        cos2 = jnp.broadcast_to(cos2, (n_rows, DIFF_DC))
        sin2 = jnp.broadcast_to(sin2, (n_rows, DIFF_DC))
    return cos2, sin2


def _token_front(x, pos_tables, w, l_params):
    cos, sin = pos_tables
    h = rmsnorm_cast(x, l_params["norm1_g"])
    dscale = DIFF_DC ** -0.5
    fscale = FOX_DH ** -0.5
    (dq,) = project(h, w["dq"], out_dtypes=(BF16,), gain=l_params["dq_g"], cos=cos, sin=sin, scale=dscale)
    dk32, dk16 = project(h, w["dk"], out_dtypes=(F32, BF16), gain=l_params["dk_g"], cos=cos, sin=sin)
    dv32, dv16 = project(h, w["dv"], out_dtypes=(F32, BF16))
    (fq,) = project(h, w["fq"], out_dtypes=(BF16,), gain=l_params["fq_g"], scale=fscale)
    fk32, fk16 = project(h, w["fk"], out_dtypes=(F32, BF16), gain=l_params["fk_g"])
    fv32, fv16 = project(h, w["fv"], out_dtypes=(F32, BF16))
    (sga,) = project(h, w["ga"], out_dtypes=(BF16,), act="sigmoid")
    (sgb,) = project(h, w["gb"], out_dtypes=(BF16,), act="sigmoid")
    return h, dict(dq=dq, dk32=dk32, dk16=dk16, dv32=dv32, dv16=dv16, fq=fq, fk32=fk32, fk16=fk16,
                   fv32=fv32, fv16=fv16, sga=sga, sgb=sgb)


def _token_back(x, od, of, t, w, l_params):
    mix = merge(od, of, w["w_a"], w["w_b"], t["sga"], t["sgb"])
    x2 = out_proj_residual(mix, w["w_o"], x)
    hb, scores = peer_query(x2, l_params["norm2_g"], w["peer_wq"], w["peer_keys"])
    n0, r1, e0, e1 = peer_topk(scores)
    return peer_mix(hb, w["peer_u"], w["peer_v"], n0, r1, e0, e1, x2)


def kernel(x_prompt, x_sample, cache_diff_k, cache_diff_v, cache_fox_k, cache_fox_v, cache_fox_logf, page_table, norm1_g, w_in, b_f, dq_g, dk_g, fq_g, fk_g, lam_q1, lam_k1, lam_q2, lam_k2, dsub_g, w_a, w_b, w_o, norm2_g, peer_wq, peer_keys, peer_u, peer_v):
    B, S, D = x_prompt.shape
    DB, DS, _ = x_sample.shape
    assert DS == 1
    depth = w_in.shape[0]
    n_pages = page_table.shape[1]
    past = n_pages * PAGE_SIZE
    n_pool = cache_diff_k.shape[1]
    WD = DIFF_HEADS * DIFF_DV
    WF = FOX_HEADS * FOX_DH
    sizes = (WD, WD, WD, WF, WF, WF, FOX_HEADS, D, D)
    names = ("dq", "dk", "dv", "fq", "fk", "fv", "fl", "ga", "gb")
    offs = [0]
    for n in sizes:
        offs.append(offs[-1] + n)

    pos_p = _rope_tables(jnp.arange(S, dtype=jnp.int32), S)
    pos_s = _rope_tables(past + jnp.arange(DS, dtype=jnp.int32), DB)

    xp = x_prompt.reshape(B * S, D)
    xs = x_sample.reshape(DB * DS, D)
    outs = [[] for _ in range(10)]
    for l in range(depth):
        lam_init = 0.8 - 0.6 * math.exp(-0.3 * l)
        w = {n: w_in[l, :, offs[i]:offs[i + 1]].astype(BF16) for i, n in enumerate(names)}
        w.update(w_a=w_a[l].astype(BF16), w_b=w_b[l].astype(BF16), w_o=w_o[l].astype(BF16),
                 peer_wq=peer_wq[l].astype(BF16),
                 peer_keys=peer_keys[l].reshape(PEER_HEADS * 2, PEER_NKEYS, PEER_DHALF).astype(BF16),
                 peer_u=peer_u[l].astype(BF16), peer_v=peer_v[l].astype(BF16))
        lp = dict(norm1_g=norm1_g[l], dq_g=dq_g[l], dk_g=dk_g[l], fq_g=fq_g[l], fk_g=fk_g[l], norm2_g=norm2_g[l])
        lam_vecs = (lam_q1[l], lam_k1[l], lam_q2[l], lam_k2[l])

        h, t = _token_front(xp, pos_p, w, lp)
        logf, c, ct = forget_gate(h, w["fl"], b_f[l], B, S)
        od = diff_attention(t["dq"], t["dk16"], t["dv16"], dsub_g[l], lam_vecs, lam_init, B, S)
        of = fox_attention(t["fq"], t["fk16"], t["fv16"], c, ct, B, S)
        xp = _token_back(xp, od, of, t, w, lp)
        outs[0].append(t["dk32"].reshape(B, S, DIFF_HEADS, DIFF_DV))
        outs[1].append(t["dv32"].reshape(B, S, DIFF_HEADS, DIFF_DV))
        outs[2].append(t["fk32"].reshape(B, S, FOX_HEADS, FOX_DH))
        outs[3].append(t["fv32"].reshape(B, S, FOX_HEADS, FOX_DH))
        outs[4].append(logf)

        hs, ts_ = _token_front(xs, pos_s, w, lp)
        slogf, _, _ = forget_gate(hs, w["fl"], b_f[l], 1, DB)
        slogf = slogf.reshape(DB, DS, FOX_HEADS)
        r3 = lambda a: a.reshape(DB, 1, a.shape[-1])
        sod, sof = decode_attention(
            page_table, r3(ts_["dq"]), r3(ts_["dk16"]), r3(ts_["dv16"]), r3(ts_["fq"]), r3(ts_["fk16"]), r3(ts_["fv16"]),
            slogf.reshape(DB, FOX_HEADS, 1),
            cache_diff_k[l].reshape(n_pool, PAGE_SIZE, WD), cache_diff_v[l].reshape(n_pool, PAGE_SIZE, WD),
            cache_fox_k[l].reshape(n_pool, PAGE_SIZE, WF), cache_fox_v[l].reshape(n_pool, PAGE_SIZE, WF),
            jnp.swapaxes(cache_fox_logf[l], 1, 2),
            dsub_g[l], lam_vecs, lam_init)
        xs = _token_back(xs, sod.reshape(DB, WD), sof.reshape(DB, WF), ts_, w, lp)
        outs[5].append(ts_["dk32"].reshape(DB, DS, DIFF_HEADS, DIFF_DV))
        outs[6].append(ts_["dv32"].reshape(DB, DS, DIFF_HEADS, DIFF_DV))
        outs[7].append(ts_["fk32"].reshape(DB, DS, FOX_HEADS, FOX_DH))
        outs[8].append(ts_["fv32"].reshape(DB, DS, FOX_HEADS, FOX_DH))
        outs[9].append(slogf)

    return (xp.reshape(B, S, D), xs.reshape(DB, DS, D)) + tuple(jnp.stack(o) for o in outs)
```

```python
import functools
import math

import jax
import jax.numpy as jnp
from jax import lax
from jax.experimental import pallas as pl
from jax.experimental.pallas import tpu as pltpu

DIFF_HEADS = 8
DIFF_DC = 128
DIFF_DV = 2 * DIFF_DC
FOX_HEADS = 16
FOX_DH = 128
PEER_HEADS = 8
PEER_NKEYS = 128
PEER_DHALF = 128
PEER_TOPK = 16
PAGE_SIZE = 128
ROPE_THETA = 10000.0
EPS = 1e-6
LANES = 128
VMEM_LIMIT = 56 * 1024 * 1024

F32 = jnp.float32
BF16 = jnp.bfloat16
NEG_INF = float("-inf")


def _cparams(*sem):
    return pltpu.CompilerParams(dimension_semantics=sem, vmem_limit_bytes=VMEM_LIMIT)


def _tile(n, pref):
    t = min(n, pref)
    while n % t:
        t //= 2
    return t


def _split3(x):
    hi = x.astype(BF16)
    r1 = x - hi.astype(F32)
    mid = r1.astype(BF16)
    lo = (r1 - mid.astype(F32)).astype(BF16)
    return hi, mid, lo


def _log_sigmoid(x):
    return jnp.minimum(x, 0.0) - jnp.log1p(jnp.exp(-jnp.abs(x)))


def _sigmoid(x):
    return 1.0 / (1.0 + jnp.exp(-x))


def _rmsnorm_kernel(x_ref, g_ref, o_ref):
    x = x_ref[...]
    ms = jnp.mean(x * x, axis=-1, keepdims=True)
    o_ref[...] = (x * lax.rsqrt(ms + EPS) * g_ref[...]).astype(o_ref.dtype)


def rmsnorm_cast(x, g):
    T, D = x.shape
    tm = _tile(T, 256)
    return pl.pallas_call(
        _rmsnorm_kernel,
        grid=(T // tm,),
        in_specs=[pl.BlockSpec((tm, D), lambda i: (i, 0)), pl.BlockSpec((1, D), lambda i: (0, 0))],
        out_specs=pl.BlockSpec((tm, D), lambda i: (i, 0)),
        out_shape=jax.ShapeDtypeStruct((T, D), BF16),
        compiler_params=_cparams("parallel"),
        name="rmsnorm_cast",
    )(x, g.reshape(1, D))


def _proj_kernel(*refs, norm, rope, act, scale):
    h_ref, w_ref = refs[0], refs[1]
    pos = 2
    gain_ref = cos_ref = sin_ref = None
    if norm:
        gain_ref = refs[pos]
        pos += 1
    if rope:
        cos_ref, sin_ref = refs[pos], refs[pos + 1]
        pos += 2
    out_refs = refs[pos:]
    z = jnp.dot(h_ref[...], w_ref[...], preferred_element_type=F32)
    tn = z.shape[1]
    if norm:
        for g in range(tn // LANES):
            x = z[:, g * LANES:(g + 1) * LANES]
            ms = jnp.mean(x * x, axis=-1, keepdims=True)
            y = x * lax.rsqrt(ms + EPS) * gain_ref[...]
            if rope:
                y = y * cos_ref[...] + pltpu.roll(y, LANES // 2, 1) * sin_ref[...]
            if scale != 1.0:
                y = y * scale
            for o_ref in out_refs:
                o_ref[:, g * LANES:(g + 1) * LANES] = y.astype(o_ref.dtype)
    else:
        if act == "sigmoid":
            z = _sigmoid(z)
        for o_ref in out_refs:
            o_ref[...] = z.astype(o_ref.dtype)


def project(h, w, *, out_dtypes, gain=None, cos=None, sin=None, act=None, scale=1.0, tm_pref=1024, tn_pref=512):
    T, D = h.shape
    N = w.shape[1]
    tm = _tile(T, tm_pref)
    tn = _tile(N, tn_pref)
    norm = gain is not None
    rope = cos is not None
    in_specs = [pl.BlockSpec((tm, D), lambda i, j: (i, 0)), pl.BlockSpec((D, tn), lambda i, j: (0, j))]
    args = [h, w]
    if norm:
        in_specs.append(pl.BlockSpec((1, LANES), lambda i, j: (0, 0)))
        args.append(gain.reshape(1, LANES))
    if rope:
        nrep = cos.shape[0] // tm
        in_specs += [pl.BlockSpec((tm, LANES), lambda i, j: (i % nrep, 0))] * 2
        args += [cos, sin]
    outs = pl.pallas_call(
        functools.partial(_proj_kernel, norm=norm, rope=rope, act=act, scale=scale),
        grid=(T // tm, N // tn),
        in_specs=in_specs,
        out_specs=[pl.BlockSpec((tm, tn), lambda i, j: (i, j)) for _ in out_dtypes],
        out_shape=[jax.ShapeDtypeStruct((T, N), dt) for dt in out_dtypes],
        compiler_params=_cparams("parallel", "arbitrary"),
        name="project",
    )(*args)
    return outs


def _forget_kernel(h_ref, w_ref, wt_ref, b_ref, bt_ref, logf_ref, ct_ref, ck_ref, carry, carry_t):
    s = pl.program_id(1)

    @pl.when(s == 0)
    def _():
        carry[...] = jnp.zeros_like(carry)
        carry_t[...] = jnp.zeros_like(carry_t)

    h = h_ref[...]
    ts = h.shape[0]
    fl = jnp.dot(h, w_ref[...], preferred_element_type=F32)
    logf = _log_sigmoid(fl + b_ref[...])
    logf_ref[...] = logf
    row = lax.broadcasted_iota(jnp.int32, (ts, ts), 0)
    col = lax.broadcasted_iota(jnp.int32, (ts, ts), 1)
    lower = (row >= col).astype(BF16)
    cum = sum(jnp.dot(lower, p, preferred_element_type=F32) for p in _split3(logf))
    c = cum + carry[...]
    carry[...] = c[ts - 1:ts, :]
    for hd in range(c.shape[1]):
        ck_ref[hd] = jnp.broadcast_to(c[:, hd:hd + 1], (ts, LANES))
    flt = lax.dot_general(wt_ref[...], h, (((1,), (1,)), ((), ())), preferred_element_type=F32)
    logft = _log_sigmoid(flt + bt_ref[...])
    upper = (row <= col).astype(BF16)
    cumt = sum(jnp.dot(p, upper, preferred_element_type=F32) for p in _split3(logft))
    ct = cumt + carry_t[...]
    ct_ref[...] = ct
    carry_t[...] = ct[:, ts - 1:ts]


def forget_gate(h, w_fl, b_f, B, S):
    D = h.shape[1]
    H = w_fl.shape[1]
    ts = _tile(S, 512)
    ns = S // ts
    return pl.pallas_call(
        _forget_kernel,
        grid=(B, ns),
        in_specs=[
            pl.BlockSpec((ts, D), lambda b, s: (b * ns + s, 0)),
            pl.BlockSpec((D, H), lambda b, s: (0, 0)),
            pl.BlockSpec((H, D), lambda b, s: (0, 0)),
            pl.BlockSpec((1, H), lambda b, s: (0, 0)),
            pl.BlockSpec((H, 1), lambda b, s: (0, 0)),
        ],
        out_specs=[
            pl.BlockSpec((None, ts, H), lambda b, s: (b, s, 0)),
            pl.BlockSpec((None, H, ts), lambda b, s: (b, 0, s)),
            pl.BlockSpec((None, H, ts, LANES), lambda b, s: (b, 0, s, 0)),
        ],
        out_shape=[
            jax.ShapeDtypeStruct((B, S, H), F32),
            jax.ShapeDtypeStruct((B, H, S), F32),
            jax.ShapeDtypeStruct((B, H, S, LANES), F32),
        ],
        scratch_shapes=[pltpu.VMEM((1, H), F32), pltpu.VMEM((H, 1), F32)],
        compiler_params=_cparams("parallel", "arbitrary"),
        name="forget_gate",
    )(h, w_fl, w_fl.T, b_f.reshape(1, H), b_f.reshape(H, 1))


def _lambda_value(lq1, lk1, lq2, lk2, lam_init):
    return (jnp.exp(jnp.sum(lq1 * lk1, axis=-1, keepdims=True))
            - jnp.exp(jnp.sum(lq2 * lk2, axis=-1, keepdims=True)) + lam_init)


_NT = (((1,), (1,)), ((), ()))
_TN = (((0,), (0,)), ((), ()))


def _softmax_step(s, v, m, l, acc):
    m_new = jnp.maximum(m, jnp.max(s, axis=0, keepdims=True))
    alpha = jnp.exp(m - m_new)
    p = jnp.exp(s - m_new)
    l = alpha * l + jnp.sum(p, axis=0, keepdims=True)
    acc = alpha * acc + lax.dot_general(v, p.astype(BF16), _TN, preferred_element_type=F32)
    return m_new, l, acc


def _causal_mask(s):
    krow = lax.broadcasted_iota(jnp.int32, s.shape, 0)
    qcol = lax.broadcasted_iota(jnp.int32, s.shape, 1)
    return jnp.where(krow <= qcol, s, NEG_INF)


def _diff_attn_kernel(q_ref, k_ref, v_ref, g_ref, lq1_ref, lk1_ref, lq2_ref, lk2_ref, o_ref, *, lam_init):
    qi = pl.program_id(2)
    t = q_ref.shape[0]
    q = q_ref[...]

    def chunk(j, carry, masked):
        rows = pl.ds(pl.multiple_of(j * t, t), t)
        k = k_ref[rows, :]
        v = v_ref[rows, :]
        out = []
        for c in range(2):
            s = lax.dot_general(k[:, c * DIFF_DC:(c + 1) * DIFF_DC], q[:, c * DIFF_DC:(c + 1) * DIFF_DC], _NT,
                                preferred_element_type=F32)
            if masked:
                s = _causal_mask(s)
            out.append(_softmax_step(s, v, *carry[c]))
        return tuple(out)

    init = tuple((jnp.full((1, t), NEG_INF, F32), jnp.zeros((1, t), F32), jnp.zeros((DIFF_DV, t), F32))
                 for _ in range(2))
    carry = lax.fori_loop(0, qi, lambda j, cr: chunk(j, cr, False), init)
    (_, l1, a1), (_, l2, a2) = chunk(qi, carry, True)
    lam = _lambda_value(lq1_ref[...], lk1_ref[...], lq2_ref[...], lk2_ref[...], lam_init)
    o = a1 / l1 - lam * (a2 / l2)
    ms = jnp.mean(o * o, axis=0, keepdims=True)
    o_ref[...] = ((o * lax.rsqrt(ms + EPS)).T * g_ref[...] * (1.0 - lam_init)).astype(o_ref.dtype)


def diff_attention(q, k, v, dsub_g, lam_vecs, lam_init, B, S):
    t = _tile(S, 512)
    n = S // t
    W = DIFF_DV
    qspec = pl.BlockSpec((t, W), lambda b, h, qi: (b * n + qi, h))
    kspec = pl.BlockSpec((S, W), lambda b, h, qi: (b, h))
    vec = pl.BlockSpec((1, DIFF_DC), lambda b, h, qi: (0, 0))
    return pl.pallas_call(
        functools.partial(_diff_attn_kernel, lam_init=lam_init),
        grid=(B, DIFF_HEADS, n),
        in_specs=[qspec, kspec, kspec, pl.BlockSpec((1, W), lambda b, h, qi: (0, 0)), vec, vec, vec, vec],
        out_specs=qspec,
        out_shape=jax.ShapeDtypeStruct((B * S, DIFF_HEADS * W), BF16),
        compiler_params=_cparams("parallel", "parallel", "arbitrary"),
        name="diff_attention",
    )(q, k, v, dsub_g.reshape(1, W), *[x.reshape(1, DIFF_DC) for x in lam_vecs])


def _fox_attn_kernel(q_ref, k_ref, v_ref, ct_ref, ck_ref, o_ref):
    h = pl.program_id(1)
    qi = pl.program_id(2)
    t = q_ref.shape[0]
    q = q_ref[...]
    cq = ct_ref[pl.ds(h, 1), :]

    def chunk(j, carry, masked):
        rows = pl.ds(pl.multiple_of(j * t, t), t)
        s = lax.dot_general(k_ref[rows, :], q, _NT, preferred_element_type=F32)
        ck = ck_ref[rows, :]
        s = jnp.concatenate([s[:, i * LANES:(i + 1) * LANES] + (cq[:, i * LANES:(i + 1) * LANES] - ck)
                             for i in range(t // LANES)], axis=1)
        if masked:
            s = _causal_mask(s)
        return _softmax_step(s, v_ref[rows, :], *carry)

    init = (jnp.full((1, t), NEG_INF, F32), jnp.zeros((1, t), F32), jnp.zeros((FOX_DH, t), F32))
    carry = lax.fori_loop(0, qi, lambda j, cr: chunk(j, cr, False), init)
    _, l, acc = chunk(qi, carry, True)
    o_ref[...] = (acc / l).T.astype(o_ref.dtype)


def fox_attention(q, k, v, ct, ck, B, S):
    t = _tile(S, 512)
    n = S // t
    H = FOX_HEADS
    qspec = pl.BlockSpec((t, FOX_DH), lambda b, h, qi: (b * n + qi, h))
    kspec = pl.BlockSpec((S, FOX_DH), lambda b, h, qi: (b, h))
    return pl.pallas_call(
        _fox_attn_kernel,
        grid=(B, H, n),
        in_specs=[
            qspec, kspec, kspec,
            pl.BlockSpec((None, H, t), lambda b, h, qi: (b, 0, qi)),
            pl.BlockSpec((None, None, S, LANES), lambda b, h, qi: (b, h, 0, 0)),
        ],
        out_specs=qspec,
        out_shape=jax.ShapeDtypeStruct((B * S, H * FOX_DH), BF16),
        compiler_params=_cparams("parallel", "parallel", "arbitrary"),
        name="fox_attention",
    )(q, k, v, ct, ck)


def _decode_kernel(pt_ref, qd_ref, kd_ref, vd_ref, qf_ref, kf_ref, vf_ref, slogf_ref,
                   pdk_ref, pdv_ref, pfk_ref, pfv_ref, plf_ref, ae_ref,
                   g_ref, lq1_ref, lk1_ref, lq2_ref, lk2_ref,
                   od_ref, of_ref,
                   qd_rows, md, ld, accd, qf_rows, mf, lf, accf, carry, *, lam_init):
    del pt_ref
    p = pl.program_id(1)
    HD = DIFF_HEADS
    HF = FOX_HEADS
    nt = (((1,), (1,)), ((), ()))

    @pl.when(p == 0)
    def _():
        q8 = qd_ref[...]
        lane = lax.broadcasted_iota(jnp.int32, q8.shape, 1)
        qr = jnp.concatenate([jnp.where(lane < DIFF_DC, q8, 0.0), jnp.where(lane >= DIFF_DC, q8, 0.0)], axis=0)
        qd_rows[...] = qr.astype(BF16)
        md[...] = jnp.sum(qr * jnp.concatenate([kd_ref[...], kd_ref[...]], axis=0), axis=-1, keepdims=True)
        ld[...] = jnp.ones_like(ld)
        accd[...] = jnp.concatenate([vd_ref[...], vd_ref[...]], axis=0)
        qf = qf_ref[...]
        qf_rows[...] = qf.astype(BF16)
        mf[...] = jnp.sum(qf * kf_ref[...], axis=-1, keepdims=True)
        lf[...] = jnp.ones_like(lf)
        accf[...] = vf_ref[...]
        carry[...] = jnp.zeros_like(carry)

    def update(s, heads, v2, m, l, acc):
        row = lax.broadcasted_iota(jnp.int32, s.shape, 0)
        col = lax.broadcasted_iota(jnp.int32, s.shape, 1)
        s = jnp.where(col % heads == row % heads, s, NEG_INF)
        m_prev = m[...]
        m_new = jnp.maximum(m_prev, jnp.max(s, axis=-1, keepdims=True))
        alpha = jnp.exp(m_prev - m_new)
        pr = jnp.exp(s - m_new)
        l[...] = alpha * l[...] + jnp.sum(pr, axis=-1, keepdims=True)
        acc[...] = alpha * acc[...] + jnp.dot(pr.astype(BF16), v2, preferred_element_type=F32)
        m[...] = m_new

    nd = PAGE_SIZE * HD
    kd2 = pdk_ref[...].reshape(nd, DIFF_DV).astype(BF16)
    sd = lax.dot_general(qd_rows[...], kd2, nt, preferred_element_type=F32)
    update(sd, HD, pdv_ref[...].reshape(nd, DIFF_DV).astype(BF16), md, ld, accd)

    nf = PAGE_SIZE * HF
    tn = (((0,), (0,)), ((), ()))
    b2 = sum(lax.dot_general(piece, ae_ref[...], tn, preferred_element_type=F32)
             for piece in _split3(plf_ref[...]))
    bias = slogf_ref[...] + carry[...] + b2[:, :nf]
    carry[...] = carry[...] + b2[:, nf:nf + 1]
    kf2 = pfk_ref[...].reshape(nf, FOX_DH).astype(BF16)
    sf = lax.dot_general(qf_rows[...], kf2, nt, preferred_element_type=F32) + bias
    update(sf, HF, pfv_ref[...].reshape(nf, FOX_DH).astype(BF16), mf, lf, accf)

    @pl.when(p == pl.num_programs(1) - 1)
    def _():
        lam = _lambda_value(lq1_ref[...], lk1_ref[...], lq2_ref[...], lk2_ref[...], lam_init)
        od = accd[...] / ld[...]
        o = od[:HD] - lam * od[HD:]
        ms = jnp.mean(o * o, axis=-1, keepdims=True)
        od_ref[...] = o * lax.rsqrt(ms + EPS) * g_ref[...] * (1.0 - lam_init)
        of_ref[...] = accf[...] / lf[...]


def _suffix_matrix(heads):
    r = jnp.arange(PAGE_SIZE)
    col_r = jnp.arange(PAGE_SIZE * heads) // heads
    after = r[:, None] > col_r[None, :]
    return jnp.concatenate([after, jnp.ones((PAGE_SIZE, LANES), bool)], axis=1).astype(BF16)


def decode_attention(page_table, layer, qd, kd, vd, qf, kf, vf, slogf_t, cache_dk, cache_dv, cache_fk, cache_fv,
                     cache_lf, dsub_g, lam_vecs, lam_init):
    DB, n_pages = page_table.shape
    HD, HF = DIFF_HEADS, FOX_HEADS

    def tok(h, w):
        return pl.BlockSpec((None, h, w), lambda b, p, pt: (b, 0, 0))

    def page(*tail):
        zeros = (0,) * (len(tail) + 1)
        return pl.BlockSpec((None, None, PAGE_SIZE) + tail, lambda b, p, pt: (layer, pt[b, n_pages - 1 - p]) + zeros)

    vec = pl.BlockSpec((1, DIFF_DC), lambda b, p, pt: (0, 0))
    ae = _suffix_matrix(HF)
    grid_spec = pltpu.PrefetchScalarGridSpec(
        num_scalar_prefetch=1,
        grid=(DB, n_pages),
        in_specs=[
            tok(HD, DIFF_DV), tok(HD, DIFF_DV), tok(HD, DIFF_DV), tok(HF, FOX_DH), tok(HF, FOX_DH), tok(HF, FOX_DH),
            tok(HF, 1),
            page(HD, DIFF_DV), page(HD, DIFF_DV), page(HF, FOX_DH), page(HF, FOX_DH), page(HF),
            pl.BlockSpec(ae.shape, lambda b, p, pt: (0, 0)),
            pl.BlockSpec((1, DIFF_DV), lambda b, p, pt: (0, 0)),
            vec, vec, vec, vec,
        ],
        out_specs=[tok(HD, DIFF_DV), tok(HF, FOX_DH)],
        scratch_shapes=[
            pltpu.VMEM((2 * HD, DIFF_DV), BF16), pltpu.VMEM((2 * HD, 1), F32), pltpu.VMEM((2 * HD, 1), F32),
            pltpu.VMEM((2 * HD, DIFF_DV), F32),
            pltpu.VMEM((HF, FOX_DH), BF16), pltpu.VMEM((HF, 1), F32), pltpu.VMEM((HF, 1), F32),
            pltpu.VMEM((HF, FOX_DH), F32),
            pltpu.VMEM((HF, 1), F32),
        ],
    )
    return pl.pallas_call(
        functools.partial(_decode_kernel, lam_init=lam_init),
        grid_spec=grid_spec,
        out_shape=[jax.ShapeDtypeStruct((DB, HD, DIFF_DV), F32), jax.ShapeDtypeStruct((DB, HF, FOX_DH), F32)],
        compiler_params=_cparams("parallel", "arbitrary"),
        name="decode_attention",
    )(page_table, qd, kd, vd, qf, kf, vf, slogf_t, cache_dk, cache_dv, cache_fk, cache_fv, cache_lf, ae,
      dsub_g.reshape(1, DIFF_DV), *[x.reshape(1, DIFF_DC) for x in lam_vecs])


def _merge_kernel(od_ref, of_ref, wa_ref, wb_ref, ga_ref, gb_ref, o_ref):
    ya = jnp.dot(od_ref[...], wa_ref[...], preferred_element_type=F32)
    yb = jnp.dot(of_ref[...], wb_ref[...], preferred_element_type=F32)
    o_ref[...] = (ga_ref[...].astype(F32) * ya + gb_ref[...].astype(F32) * yb).astype(o_ref.dtype)


def merge(od, of, w_a, w_b, sga, sgb):
    T, Wd = od.shape
    Wf = of.shape[1]
    D = w_a.shape[1]
    tm = _tile(T, 1024)
    tn = _tile(D, 512)
    return pl.pallas_call(
        _merge_kernel,
        grid=(T // tm, D // tn),
        in_specs=[
            pl.BlockSpec((tm, Wd), lambda i, j: (i, 0)), pl.BlockSpec((tm, Wf), lambda i, j: (i, 0)),
            pl.BlockSpec((Wd, tn), lambda i, j: (0, j)), pl.BlockSpec((Wf, tn), lambda i, j: (0, j)),
            pl.BlockSpec((tm, tn), lambda i, j: (i, j)), pl.BlockSpec((tm, tn), lambda i, j: (i, j)),
        ],
        out_specs=pl.BlockSpec((tm, tn), lambda i, j: (i, j)),
        out_shape=jax.ShapeDtypeStruct((T, D), BF16),
        compiler_params=_cparams("parallel", "arbitrary"),
        name="merge",
    )(od, of, w_a, w_b, sga, sgb)


def _out_proj_kernel(mix_ref, w_ref, x_ref, o_ref):
    o_ref[...] = x_ref[...] + jnp.dot(mix_ref[...], w_ref[...], preferred_element_type=F32)


def out_proj_residual(mix, w_o, x):
    T, D = mix.shape
    N = w_o.shape[1]
    tm = _tile(T, 1024)
    tn = _tile(N, 512)
    return pl.pallas_call(
        _out_proj_kernel,
        grid=(T // tm, N // tn),
        in_specs=[
            pl.BlockSpec((tm, D), lambda i, j: (i, 0)), pl.BlockSpec((D, tn), lambda i, j: (0, j)),
            pl.BlockSpec((tm, tn), lambda i, j: (i, j)),
        ],
        out_specs=pl.BlockSpec((tm, tn), lambda i, j: (i, j)),
        out_shape=jax.ShapeDtypeStruct((T, N), F32),
        compiler_params=_cparams("parallel", "arbitrary"),
        name="out_proj_residual",
    )(mix, w_o, x)


def _peer_query_kernel(x_ref, g_ref, wq_ref, keys_ref, hb_ref, st_ref, hb_scr):
    j = pl.program_id(1)

    @pl.when(j == 0)
    def _():
        x = x_ref[...]
        ms = jnp.mean(x * x, axis=-1, keepdims=True)
        hb = (x * lax.rsqrt(ms + EPS) * g_ref[...]).astype(BF16)
        hb_scr[...] = hb
        hb_ref[...] = hb

    q = jnp.dot(hb_scr[...], wq_ref[...], preferred_element_type=F32).astype(BF16)
    for g in range(q.shape[1] // PEER_DHALF):
        st_ref[g] = lax.dot_general(keys_ref[g], q[:, g * PEER_DHALF:(g + 1) * PEER_DHALF],
                                    (((1,), (1,)), ((), ())), preferred_element_type=F32)


def peer_query(x, g, wq, keys):
    T, D = x.shape
    NQ = wq.shape[1]
    tm = _tile(T, 512)
    gpt = 4
    tn = gpt * PEER_DHALF
    return pl.pallas_call(
        _peer_query_kernel,
        grid=(T // tm, NQ // tn),
        in_specs=[
            pl.BlockSpec((tm, D), lambda i, j: (i, 0)), pl.BlockSpec((1, D), lambda i, j: (0, 0)),
            pl.BlockSpec((D, tn), lambda i, j: (0, j)),
            pl.BlockSpec((gpt, PEER_NKEYS, PEER_DHALF), lambda i, j: (j, 0, 0)),
        ],
        out_specs=[pl.BlockSpec((tm, D), lambda i, j: (i, 0)),
                   pl.BlockSpec((gpt, PEER_NKEYS, tm), lambda i, j: (j, 0, i))],
        out_shape=[jax.ShapeDtypeStruct((T, D), BF16),
                   jax.ShapeDtypeStruct((NQ // PEER_DHALF, PEER_NKEYS, T), F32)],
        scratch_shapes=[pltpu.VMEM((tm, D), BF16)],
        compiler_params=_cparams("parallel", "arbitrary"),
        name="peer_query",
    )(x, g.reshape(1, D), wq, keys)


def _topk_rows(xs, n):
    R, tt = xs[0].shape
    rows = lax.broadcasted_iota(jnp.int32, (R, tt), 0).astype(F32)
    slot = lax.broadcasted_iota(jnp.int32, (n, tt), 0)

    def body(r, carry):
        out = []
        for x, rank, vals in carry:
            m = jnp.max(x, axis=0, keepdims=True)
            idx = jnp.min(jnp.where(x == m, rows, float(R)), axis=0, keepdims=True)
            hit = rows == idx
            out.append((jnp.where(hit, NEG_INF, x), jnp.where(hit, r.astype(F32), rank),
                        jnp.where(slot == r, m, vals)))
        return tuple(out)

    init = tuple((x, jnp.full((R, tt), float(n), F32), jnp.zeros((n, tt), F32)) for x in xs)
    res = lax.fori_loop(0, n, body, init)
    return [(rank, vals) for _, rank, vals in res]


def _peer_topk_kernel(s_ref, n0_ref, e0_ref, r1_ref, e1_ref):
    K = PEER_TOPK
    KB = K // 2

    def head(h, carry):
        s0 = s_ref[2 * h]
        s1 = s_ref[2 * h + 1]
        (rank0, v0), (rank1, v1) = _topk_rows((s0, s1), K)
        cand = jnp.concatenate([v0[0:1, :] + v1] + [v0[a:a + 1, :] + v1[0:KB, :] for a in range(1, K)], axis=0)
        row = lax.broadcasted_iota(jnp.int32, cand.shape, 0)
        ra = jnp.where(row < K, 0, (row - K) // KB + 1)
        rb = jnp.where(row < K, row, (row - K) % KB)
        cand = jnp.where((ra + 1) * (rb + 1) <= K, cand, NEG_INF)
        ((crank, cvals),) = _topk_rows((cand,), K)
        sel = crank < float(K)
        z = jnp.sum(jnp.where(sel, jnp.exp(cand - cvals[0:1, :]), 0.0), axis=0, keepdims=True)
        n0 = jnp.zeros_like(s0)
        for a in range(K):
            lo, hi = (0, K) if a == 0 else (K + KB * (a - 1), K + KB * a)
            cnt = jnp.sum(jnp.where(sel[lo:hi, :], 1.0, 0.0), axis=0, keepdims=True)
            n0 = jnp.where(rank0 == float(a), cnt, n0)
        n0_ref[h] = n0
        e0_ref[h] = jnp.exp(s0 - v0[0:1, :])
        r1_ref[h] = rank1.astype(r1_ref.dtype)
        e1_ref[h] = (jnp.exp(s1 - v1[0:1, :]) / z).astype(e1_ref.dtype)
        return carry

    lax.fori_loop(0, s_ref.shape[0] // 2, head, 0)


def peer_topk(scores_t):
    G, NK, T = scores_t.shape
    tt = _tile(T, LANES)
    spec = pl.BlockSpec((G // 2, NK, tt), lambda i: (0, 0, i))
    return pl.pallas_call(
        _peer_topk_kernel,
        grid=(T // tt,),
        in_specs=[pl.BlockSpec((G, NK, tt), lambda i: (0, 0, i))],
        out_specs=[spec] * 4,
        out_shape=[jax.ShapeDtypeStruct((G // 2, NK, T), dt) for dt in (F32, F32, BF16, BF16)],
        compiler_params=_cparams("parallel"),
        name="peer_topk",
    )(scores_t)


def _peer_mix_kernel(hb_ref, u_ref, v_ref, n0_ref, e0_ref, r1_ref, e1_ref, x_ref, o_ref):
    e = pl.program_id(1)

    @pl.when(e == 0)
    def _():
        o_ref[...] = x_ref[...]

    NK = PEER_NKEYS
    ipc = u_ref.shape[0] // NK
    at = lax.dot_general(u_ref[...], hb_ref[...], (((1,), (1,)), ((), ())), preferred_element_type=F32)
    act = (0.5 * at * (1.0 + lax.erf(at * (2.0 ** -0.5)))).astype(BF16)
    zero = jnp.zeros((), r1_ref.dtype)
    parts = []
    for ii in range(ipc):
        i = e * ipc + ii
        gate = None
        for h in range(n0_ref.shape[0]):
            n0row = n0_ref[h, pl.ds(i, 1), :].astype(r1_ref.dtype)
            e0row = e0_ref[h, pl.ds(i, 1), :].astype(e1_ref.dtype)
            term = jnp.where(r1_ref[h] < n0row, e1_ref[h], zero) * e0row
            gate = term if gate is None else gate + term
        parts.append(gate * act[ii * NK:(ii + 1) * NK, :])
    wt = jnp.concatenate(parts, axis=0)
    o_ref[...] += lax.dot_general(wt, v_ref[...], (((0,), (0,)), ((), ())), preferred_element_type=F32)


def peer_mix(hb, u_tab, v_tab, n0, e0, r1, e1, x):
    T, D = hb.shape
    NE = u_tab.shape[0]
    heads, NK, _ = n0.shape
    tm = _tile(T, 512)
    ec = 2 * NK
    once = dict(pipeline_mode=pl.Buffered(1))
    tok_in = functools.partial(pl.BlockSpec, (tm, D), lambda i, e: (i, 0))
    tab = pl.BlockSpec((ec, D), lambda i, e: (e, 0))
    desc = pl.BlockSpec((heads, NK, tm), lambda i, e: (0, 0, i), **once)
    return pl.pallas_call(
        _peer_mix_kernel,
        grid=(T // tm, NE // ec),
        in_specs=[tok_in(**once), tab, tab, desc, desc, desc, desc, tok_in(**once)],
        out_specs=pl.BlockSpec((tm, D), lambda i, e: (i, 0)),
        out_shape=jax.ShapeDtypeStruct((T, D), F32),
        compiler_params=_cparams("parallel", "arbitrary"),
        name="peer_mix",
    )(hb, u_tab, v_tab, n0, e0, r1, e1, x)


def _rope_tables(pos, n_rows):
    half = DIFF_DC // 2
    inv = ROPE_THETA ** (-jnp.arange(half, dtype=F32) / half)
    ang = pos.astype(F32)[:, None] * inv[None, :]
    cos = jnp.cos(ang)
    sin = jnp.sin(ang)
    cos2 = jnp.concatenate([cos, cos], axis=-1)
    sin2 = jnp.concatenate([-sin, sin], axis=-1)
    if cos2.shape[0] != n_rows:
        cos2 = jnp.broadcast_to(cos2, (n_rows, DIFF_DC))
        sin2 = jnp.broadcast_to(sin2, (n_rows, DIFF_DC))
    return cos2, sin2


def _token_front(x, pos_tables, w, l_params):
    cos, sin = pos_tables
    h = rmsnorm_cast(x, l_params["norm1_g"])
    dscale = DIFF_DC ** -0.5
    fscale = FOX_DH ** -0.5
    (dq,) = project(h, w["dq"], out_dtypes=(BF16,), gain=l_params["dq_g"], cos=cos, sin=sin, scale=dscale)
    dk32, dk16 = project(h, w["dk"], out_dtypes=(F32, BF16), gain=l_params["dk_g"], cos=cos, sin=sin)
    dv32, dv16 = project(h, w["dv"], out_dtypes=(F32, BF16))
    (fq,) = project(h, w["fq"], out_dtypes=(BF16,), gain=l_params["fq_g"], scale=fscale)
    fk32, fk16 = project(h, w["fk"], out_dtypes=(F32, BF16), gain=l_params["fk_g"])
    fv32, fv16 = project(h, w["fv"], out_dtypes=(F32, BF16))
    (sga,) = project(h, w["ga"], out_dtypes=(BF16,), act="sigmoid")
    (sgb,) = project(h, w["gb"], out_dtypes=(BF16,), act="sigmoid")
    return h, dict(dq=dq, dk32=dk32, dk16=dk16, dv32=dv32, dv16=dv16, fq=fq, fk32=fk32, fk16=fk16,
                   fv32=fv32, fv16=fv16, sga=sga, sgb=sgb)


def _token_back(x, od, of, t, w, l_params):
    mix = merge(od, of, w["w_a"], w["w_b"], t["sga"], t["sgb"])
    x2 = out_proj_residual(mix, w["w_o"], x)
    hb, scores_t = peer_query(x2, l_params["norm2_g"], w["peer_wq"], w["peer_keys"])
    n0, e0, r1, e1 = peer_topk(scores_t)
    return peer_mix(hb, w["peer_u"], w["peer_v"], n0, e0, r1, e1, x2)


def kernel(x_prompt, x_sample, cache_diff_k, cache_diff_v, cache_fox_k, cache_fox_v, cache_fox_logf, page_table, norm1_g, w_in, b_f, dq_g, dk_g, fq_g, fk_g, lam_q1, lam_k1, lam_q2, lam_k2, dsub_g, w_a, w_b, w_o, norm2_g, peer_wq, peer_keys, peer_u, peer_v):
    B, S, D = x_prompt.shape
    DB, DS, _ = x_sample.shape
    assert DS == 1
    depth = w_in.shape[0]
    n_pages = page_table.shape[1]
    past = n_pages * PAGE_SIZE
    WD = DIFF_HEADS * DIFF_DV
    WF = FOX_HEADS * FOX_DH
    sizes = (WD, WD, WD, WF, WF, WF, FOX_HEADS, D, D)
    names = ("dq", "dk", "dv", "fq", "fk", "fv", "fl", "ga", "gb")
    offs = [0]
    for n in sizes:
        offs.append(offs[-1] + n)

    pos_p = _rope_tables(jnp.arange(S, dtype=jnp.int32), S)
    pos_s = _rope_tables(past + jnp.arange(DS, dtype=jnp.int32), DB)

    xp = x_prompt.reshape(B * S, D)
    xs = x_sample.reshape(DB * DS, D)
    outs = [[] for _ in range(10)]
    for l in range(depth):
        lam_init = 0.8 - 0.6 * math.exp(-0.3 * l)
        w = {n: w_in[l, :, offs[i]:offs[i + 1]].astype(BF16) for i, n in enumerate(names)}
        w.update(w_a=w_a[l].astype(BF16), w_b=w_b[l].astype(BF16), w_o=w_o[l].astype(BF16),
                 peer_wq=peer_wq[l].astype(BF16),
                 peer_keys=peer_keys[l].reshape(PEER_HEADS * 2, PEER_NKEYS, PEER_DHALF).astype(BF16),
                 peer_u=peer_u[l].astype(BF16), peer_v=peer_v[l].astype(BF16))
        lp = dict(norm1_g=norm1_g[l], dq_g=dq_g[l], dk_g=dk_g[l], fq_g=fq_g[l], fk_g=fk_g[l], norm2_g=norm2_g[l])
        lam_vecs = (lam_q1[l], lam_k1[l], lam_q2[l], lam_k2[l])

        h, t = _token_front(xp, pos_p, w, lp)
        logf, ct, ck = forget_gate(h, w["fl"], b_f[l], B, S)
        od = diff_attention(t["dq"], t["dk16"], t["dv16"], dsub_g[l], lam_vecs, lam_init, B, S)
        of = fox_attention(t["fq"], t["fk16"], t["fv16"], ct, ck, B, S)
        xp = _token_back(xp, od, of, t, w, lp)
        outs[0].append(t["dk32"].reshape(B, S, DIFF_HEADS, DIFF_DV))
        outs[1].append(t["dv32"].reshape(B, S, DIFF_HEADS, DIFF_DV))
        outs[2].append(t["fk32"].reshape(B, S, FOX_HEADS, FOX_DH))
        outs[3].append(t["fv32"].reshape(B, S, FOX_HEADS, FOX_DH))
        outs[4].append(logf)

        hs, ts_ = _token_front(xs, pos_s, w, lp)
        slogf, _, _ = forget_gate(hs, w["fl"], b_f[l], 1, DB)
        slogf = slogf.reshape(DB, DS, FOX_HEADS)
        tokd = lambda a: a.astype(F32).reshape(DB, DIFF_HEADS, DIFF_DV)
        tokf = lambda a: a.astype(F32).reshape(DB, FOX_HEADS, FOX_DH)
        sod, sof = decode_attention(
            page_table, l, tokd(ts_["dq"]), tokd(ts_["dk16"]), tokd(ts_["dv16"]),
            tokf(ts_["fq"]), tokf(ts_["fk16"]), tokf(ts_["fv16"]), slogf.reshape(DB, FOX_HEADS, 1),
            cache_diff_k, cache_diff_v, cache_fox_k, cache_fox_v, cache_fox_logf,
            dsub_g[l], lam_vecs, lam_init)
        sod = sod.astype(BF16)
        sof = sof.astype(BF16)
        xs = _token_back(xs, sod.reshape(DB, WD), sof.reshape(DB, WF), ts_, w, lp)
        outs[5].append(ts_["dk32"].reshape(DB, DS, DIFF_HEADS, DIFF_DV))
        outs[6].append(ts_["dv32"].reshape(DB, DS, DIFF_HEADS, DIFF_DV))
        outs[7].append(ts_["fk32"].reshape(DB, DS, FOX_HEADS, FOX_DH))
        outs[8].append(ts_["fv32"].reshape(DB, DS, FOX_HEADS, FOX_DH))
        outs[9].append(slogf)

    return (xp.reshape(B, S, D), xs.reshape(DB, DS, D)) + tuple(jnp.stack(o) for o in outs)
```

```python
import functools
import math

import jax
import jax.numpy as jnp
from jax import lax
from jax.experimental import pallas as pl
from jax.experimental.pallas import tpu as pltpu

DIFF_HEADS = 8
DIFF_DC = 128
DIFF_DV = 2 * DIFF_DC
FOX_HEADS = 16
FOX_DH = 128
PEER_HEADS = 8
PEER_NKEYS = 128
PEER_DHALF = 128
PEER_TOPK = 16
PAGE_SIZE = 128
ROPE_THETA = 10000.0
EPS = 1e-6
LANES = 128
VMEM_LIMIT = 56 * 1024 * 1024

F32 = jnp.float32
BF16 = jnp.bfloat16
NEG_INF = float("-inf")


def _cparams(*sem):
    return pltpu.CompilerParams(dimension_semantics=sem, vmem_limit_bytes=VMEM_LIMIT)


def _tile(n, pref):
    t = min(n, pref)
    while n % t:
        t //= 2
    return t


def _split3(x):
    hi = x.astype(BF16)
    r1 = x - hi.astype(F32)
    mid = r1.astype(BF16)
    lo = (r1 - mid.astype(F32)).astype(BF16)
    return hi, mid, lo


def _log_sigmoid(x):
    return jnp.minimum(x, 0.0) - jnp.log1p(jnp.exp(-jnp.abs(x)))


def _sigmoid(x):
    return 1.0 / (1.0 + jnp.exp(-x))


def _rmsnorm_kernel(x_ref, g_ref, o_ref):
    x = x_ref[...]
    ms = jnp.mean(x * x, axis=-1, keepdims=True)
    o_ref[...] = (x * lax.rsqrt(ms + EPS) * g_ref[...]).astype(o_ref.dtype)


def rmsnorm_cast(x, g):
    T, D = x.shape
    tm = _tile(T, 256)
    return pl.pallas_call(
        _rmsnorm_kernel,
        grid=(T // tm,),
        in_specs=[pl.BlockSpec((tm, D), lambda i: (i, 0)), pl.BlockSpec((1, D), lambda i: (0, 0))],
        out_specs=pl.BlockSpec((tm, D), lambda i: (i, 0)),
        out_shape=jax.ShapeDtypeStruct((T, D), BF16),
        compiler_params=_cparams("parallel"),
        name="rmsnorm_cast",
    )(x, g.reshape(1, D))


def _proj_kernel(*refs, norm, rope, act, scale):
    h_ref, w_ref = refs[0], refs[1]
    pos = 2
    gain_ref = cos_ref = sin_ref = None
    if norm:
        gain_ref = refs[pos]
        pos += 1
    if rope:
        cos_ref, sin_ref = refs[pos], refs[pos + 1]
        pos += 2
    out_refs = refs[pos:]
    z = jnp.dot(h_ref[...], w_ref[...], preferred_element_type=F32)
    tn = z.shape[1]
    if norm:
        for g in range(tn // LANES):
            x = z[:, g * LANES:(g + 1) * LANES]
            ms = jnp.mean(x * x, axis=-1, keepdims=True)
            y = x * lax.rsqrt(ms + EPS) * gain_ref[...]
            if rope:
                y = y * cos_ref[...] + pltpu.roll(y, LANES // 2, 1) * sin_ref[...]
            if scale != 1.0:
                y = y * scale
            for o_ref in out_refs:
                o_ref[:, g * LANES:(g + 1) * LANES] = y.astype(o_ref.dtype)
    else:
        if act == "sigmoid":
            z = _sigmoid(z)
        for o_ref in out_refs:
            o_ref[...] = z.astype(o_ref.dtype)


def project(h, w, *, out_dtypes, gain=None, cos=None, sin=None, act=None, scale=1.0, tm_pref=1024, tn_pref=512):
    T, D = h.shape
    N = w.shape[1]
    tm = _tile(T, tm_pref)
    tn = _tile(N, tn_pref)
    norm = gain is not None
    rope = cos is not None
    in_specs = [pl.BlockSpec((tm, D), lambda i, j: (i, 0)), pl.BlockSpec((D, tn), lambda i, j: (0, j))]
    args = [h, w]
    if norm:
        in_specs.append(pl.BlockSpec((1, LANES), lambda i, j: (0, 0)))
        args.append(gain.reshape(1, LANES))
    if rope:
        nrep = cos.shape[0] // tm
        in_specs += [pl.BlockSpec((tm, LANES), lambda i, j: (i % nrep, 0))] * 2
        args += [cos, sin]
    outs = pl.pallas_call(
        functools.partial(_proj_kernel, norm=norm, rope=rope, act=act, scale=scale),
        grid=(T // tm, N // tn),
        in_specs=in_specs,
        out_specs=[pl.BlockSpec((tm, tn), lambda i, j: (i, j)) for _ in out_dtypes],
        out_shape=[jax.ShapeDtypeStruct((T, N), dt) for dt in out_dtypes],
        compiler_params=_cparams("parallel", "arbitrary"),
        name="project",
    )(*args)
    return outs


def _forget_kernel(h_ref, w_ref, wt_ref, b_ref, bt_ref, logf_ref, ct_ref, ck_ref, carry, carry_t):
    s = pl.program_id(1)

    @pl.when(s == 0)
    def _():
        carry[...] = jnp.zeros_like(carry)
        carry_t[...] = jnp.zeros_like(carry_t)

    h = h_ref[...]
    ts = h.shape[0]
    fl = jnp.dot(h, w_ref[...], preferred_element_type=F32)
    logf = _log_sigmoid(fl + b_ref[...])
    logf_ref[...] = logf
    row = lax.broadcasted_iota(jnp.int32, (ts, ts), 0)
    col = lax.broadcasted_iota(jnp.int32, (ts, ts), 1)
    lower = (row >= col).astype(BF16)
    cum = sum(jnp.dot(lower, p, preferred_element_type=F32) for p in _split3(logf))
    c = cum + carry[...]
    carry[...] = c[ts - 1:ts, :]
    for hd in range(c.shape[1]):
        ck_ref[hd] = jnp.broadcast_to(c[:, hd:hd + 1], (ts, LANES))
    flt = lax.dot_general(wt_ref[...], h, (((1,), (1,)), ((), ())), preferred_element_type=F32)
    logft = _log_sigmoid(flt + bt_ref[...])
    upper = (row <= col).astype(BF16)
    cumt = sum(jnp.dot(p, upper, preferred_element_type=F32) for p in _split3(logft))
    ct = cumt + carry_t[...]
    ct_ref[...] = ct
    carry_t[...] = ct[:, ts - 1:ts]


def forget_gate(h, w_fl, b_f, B, S):
    D = h.shape[1]
    H = w_fl.shape[1]
    ts = _tile(S, 512)
    ns = S // ts
    return pl.pallas_call(
        _forget_kernel,
        grid=(B, ns),
        in_specs=[
            pl.BlockSpec((ts, D), lambda b, s: (b * ns + s, 0)),
            pl.BlockSpec((D, H), lambda b, s: (0, 0)),
            pl.BlockSpec((H, D), lambda b, s: (0, 0)),
            pl.BlockSpec((1, H), lambda b, s: (0, 0)),
            pl.BlockSpec((H, 1), lambda b, s: (0, 0)),
        ],
        out_specs=[
            pl.BlockSpec((None, ts, H), lambda b, s: (b, s, 0)),
            pl.BlockSpec((None, H, ts), lambda b, s: (b, 0, s)),
            pl.BlockSpec((None, H, ts, LANES), lambda b, s: (b, 0, s, 0)),
        ],
        out_shape=[
            jax.ShapeDtypeStruct((B, S, H), F32),
            jax.ShapeDtypeStruct((B, H, S), F32),
            jax.ShapeDtypeStruct((B, H, S, LANES), F32),
        ],
        scratch_shapes=[pltpu.VMEM((1, H), F32), pltpu.VMEM((H, 1), F32)],
        compiler_params=_cparams("parallel", "arbitrary"),
        name="forget_gate",
    )(h, w_fl, w_fl.T, b_f.reshape(1, H), b_f.reshape(H, 1))


def _lambda_value(lq1, lk1, lq2, lk2, lam_init):
    return (jnp.exp(jnp.sum(lq1 * lk1, axis=-1, keepdims=True))
            - jnp.exp(jnp.sum(lq2 * lk2, axis=-1, keepdims=True)) + lam_init)


_NT = (((1,), (1,)), ((), ()))
_TN = (((0,), (0,)), ((), ()))


def _softmax_step(s, v, m, l, acc):
    m_new = jnp.maximum(m, jnp.max(s, axis=0, keepdims=True))
    alpha = jnp.exp(m - m_new)
    p = jnp.exp(s - m_new)
    l = alpha * l + jnp.sum(p, axis=0, keepdims=True)
    acc = alpha * acc + lax.dot_general(v, p.astype(BF16), _TN, preferred_element_type=F32)
    return m_new, l, acc


def _causal_mask(s, key_offset):
    krow = lax.broadcasted_iota(jnp.int32, s.shape, 0)
    qcol = lax.broadcasted_iota(jnp.int32, s.shape, 1)
    return jnp.where(krow + key_offset <= qcol, s, NEG_INF)


def _causal_sweep(chunk, init, qi, tq, tk):
    n_full = qi * (tq // tk)
    carry = lax.fori_loop(0, n_full, lambda j, cr: chunk(j, cr, None), init)
    for d in range(tq // tk):
        carry = chunk(n_full + d, carry, d * tk)
    return carry


ATTN_TQ = 1024
ATTN_TK = 1024


def _diff_attn_kernel(q_ref, k_ref, v_ref, g_ref, lq1_ref, lk1_ref, lq2_ref, lk2_ref, o_ref, *, lam_init, tk):
    qi = pl.program_id(2)
    tq = q_ref.shape[0]
    q = q_ref[...]

    def chunk(j, carry, key_offset):
        rows = pl.ds(pl.multiple_of(j * tk, tk), tk)
        k = k_ref[rows, :]
        v = v_ref[rows, :]
        out = []
        for c in range(2):
            s = lax.dot_general(k[:, c * DIFF_DC:(c + 1) * DIFF_DC], q[:, c * DIFF_DC:(c + 1) * DIFF_DC], _NT,
                                preferred_element_type=F32)
            if key_offset is not None:
                s = _causal_mask(s, key_offset)
            out.append(_softmax_step(s, v, *carry[c]))
        return tuple(out)

    init = tuple((jnp.full((1, tq), NEG_INF, F32), jnp.zeros((1, tq), F32), jnp.zeros((DIFF_DV, tq), F32))
                 for _ in range(2))
    (_, l1, a1), (_, l2, a2) = _causal_sweep(chunk, init, qi, tq, tk)
    lam = _lambda_value(lq1_ref[...], lk1_ref[...], lq2_ref[...], lk2_ref[...], lam_init)
    o = a1 / l1 - lam * (a2 / l2)
    ms = jnp.mean(o * o, axis=0, keepdims=True)
    o_ref[...] = ((o * lax.rsqrt(ms + EPS)).T * g_ref[...] * (1.0 - lam_init)).astype(o_ref.dtype)


def diff_attention(q, k, v, dsub_g, lam_vecs, lam_init, B, S):
    tq = _tile(S, ATTN_TQ)
    tk = _tile(tq, ATTN_TK)
    n = S // tq
    W = DIFF_DV
    qspec = pl.BlockSpec((tq, W), lambda b, h, qi: (b * n + qi, h))
    kspec = pl.BlockSpec((S, W), lambda b, h, qi: (b, h))
    vec = pl.BlockSpec((1, DIFF_DC), lambda b, h, qi: (0, 0))
    return pl.pallas_call(
        functools.partial(_diff_attn_kernel, lam_init=lam_init, tk=tk),
        grid=(B, DIFF_HEADS, n),
        in_specs=[qspec, kspec, kspec, pl.BlockSpec((1, W), lambda b, h, qi: (0, 0)), vec, vec, vec, vec],
        out_specs=qspec,
        out_shape=jax.ShapeDtypeStruct((B * S, DIFF_HEADS * W), BF16),
        compiler_params=_cparams("parallel", "parallel", "arbitrary"),
        name="diff_attention",
    )(q, k, v, dsub_g.reshape(1, W), *[x.reshape(1, DIFF_DC) for x in lam_vecs])


def _fox_attn_kernel(q_ref, k_ref, v_ref, ct_ref, ck_ref, o_ref, *, tk):
    h = pl.program_id(1)
    qi = pl.program_id(2)
    tq = q_ref.shape[0]
    q = q_ref[...]
    cq = ct_ref[pl.ds(h, 1), :]

    def chunk(j, carry, key_offset):
        rows = pl.ds(pl.multiple_of(j * tk, tk), tk)
        s = lax.dot_general(k_ref[rows, :], q, _NT, preferred_element_type=F32)
        ck = ck_ref[rows, :]
        s = jnp.concatenate([s[:, i * LANES:(i + 1) * LANES] + (cq[:, i * LANES:(i + 1) * LANES] - ck)
                             for i in range(tq // LANES)], axis=1)
        if key_offset is not None:
            s = _causal_mask(s, key_offset)
        return _softmax_step(s, v_ref[rows, :], *carry)

    init = (jnp.full((1, tq), NEG_INF, F32), jnp.zeros((1, tq), F32), jnp.zeros((FOX_DH, tq), F32))
    _, l, acc = _causal_sweep(chunk, init, qi, tq, tk)
    o_ref[...] = (acc / l).T.astype(o_ref.dtype)


def fox_attention(q, k, v, ct, ck, B, S):
    tq = _tile(S, ATTN_TQ)
    tk = _tile(tq, ATTN_TK)
    n = S // tq
    H = FOX_HEADS
    qspec = pl.BlockSpec((tq, FOX_DH), lambda b, h, qi: (b * n + qi, h))
    kspec = pl.BlockSpec((S, FOX_DH), lambda b, h, qi: (b, h))
    return pl.pallas_call(
        functools.partial(_fox_attn_kernel, tk=tk),
        grid=(B, H, n),
        in_specs=[
            qspec, kspec, kspec,
            pl.BlockSpec((None, H, tq), lambda b, h, qi: (b, 0, qi)),
            pl.BlockSpec((None, None, S, LANES), lambda b, h, qi: (b, h, 0, 0)),
        ],
        out_specs=qspec,
        out_shape=jax.ShapeDtypeStruct((B * S, H * FOX_DH), BF16),
        compiler_params=_cparams("parallel", "parallel", "arbitrary"),
        name="fox_attention",
    )(q, k, v, ct, ck)


def _decode_kernel(*refs, lam_init, npp):
    pt_ref, td_ref, tf_ref, slogf_ref = refs[:4]
    pages = refs[4:4 + 5 * npp]
    ae_ref, g_ref, lamv_ref, od_ref, of_ref = refs[4 + 5 * npp:9 + 5 * npp]
    qd_rows, md, ld, accd, qf_rows, mf, lf, accf, carry = refs[9 + 5 * npp:]
    del pt_ref
    p = pl.program_id(1)
    HD = DIFF_HEADS
    HF = FOX_HEADS

    @pl.when(p == 0)
    def _():
        q8 = td_ref[0]
        lane = lax.broadcasted_iota(jnp.int32, q8.shape, 1)
        qr = jnp.concatenate([jnp.where(lane < DIFF_DC, q8, 0.0), jnp.where(lane >= DIFF_DC, q8, 0.0)], axis=0)
        qd_rows[...] = qr.astype(BF16)
        md[...] = jnp.sum(qr * jnp.concatenate([td_ref[1], td_ref[1]], axis=0), axis=-1, keepdims=True)
        ld[...] = jnp.ones_like(ld)
        accd[...] = jnp.concatenate([td_ref[2], td_ref[2]], axis=0)
        qf = tf_ref[0]
        qf_rows[...] = qf.astype(BF16)
        mf[...] = jnp.sum(qf * tf_ref[1], axis=-1, keepdims=True)
        lf[...] = jnp.ones_like(lf)
        accf[...] = tf_ref[2]
        carry[...] = jnp.zeros_like(carry)

    def update(scores, heads, values, m, l, acc):
        row = lax.broadcasted_iota(jnp.int32, scores[0].shape, 0)
        col = lax.broadcasted_iota(jnp.int32, scores[0].shape, 1)
        own = col % heads == row % heads
        scores = [jnp.where(own, s, NEG_INF) for s in scores]
        m_prev = m[...]
        m_new = m_prev
        for s in scores:
            m_new = jnp.maximum(m_new, jnp.max(s, axis=-1, keepdims=True))
        alpha = jnp.exp(m_prev - m_new)
        l_new = alpha * l[...]
        acc_new = alpha * acc[...]
        for s, v2 in zip(scores, values):
            pr = jnp.exp(s - m_new)
            l_new = l_new + jnp.sum(pr, axis=-1, keepdims=True)
            acc_new = acc_new + jnp.dot(pr.astype(BF16), v2, preferred_element_type=F32)
        l[...] = l_new
        acc[...] = acc_new
        m[...] = m_new

    nd = PAGE_SIZE * HD
    nf = PAGE_SIZE * HF
    sd, vd, sf, vf = [], [], [], []
    total = carry[...]
    for i in range(npp):
        pdk_ref, pdv_ref, pfk_ref, pfv_ref, plf_ref = pages[5 * i:5 * i + 5]
        kd2 = pdk_ref[...].reshape(nd, DIFF_DV).astype(BF16)
        sd.append(lax.dot_general(qd_rows[...], kd2, _NT, preferred_element_type=F32))
        vd.append(pdv_ref[...].reshape(nd, DIFF_DV).astype(BF16))
        b2 = sum(lax.dot_general(piece, ae_ref[...], _TN, preferred_element_type=F32)
                 for piece in _split3(plf_ref[...]))
        bias = slogf_ref[...] + total + b2[:, :nf]
        total = total + b2[:, nf:nf + 1]
        kf2 = pfk_ref[...].reshape(nf, FOX_DH).astype(BF16)
        sf.append(lax.dot_general(qf_rows[...], kf2, _NT, preferred_element_type=F32) + bias)
        vf.append(pfv_ref[...].reshape(nf, FOX_DH).astype(BF16))
    carry[...] = total
    update(sd, HD, vd, md, ld, accd)
    update(sf, HF, vf, mf, lf, accf)

    @pl.when(p == pl.num_programs(1) - 1)
    def _():
        lam = _lambda_value(lamv_ref[0:1, :], lamv_ref[1:2, :], lamv_ref[2:3, :], lamv_ref[3:4, :], lam_init)
        od = accd[...] / ld[...]
        o = od[:HD] - lam * od[HD:]
        ms = jnp.mean(o * o, axis=-1, keepdims=True)
        od_ref[...] = o * lax.rsqrt(ms + EPS) * g_ref[...] * (1.0 - lam_init)
        of_ref[...] = accf[...] / lf[...]


def _suffix_matrix(heads):
    r = jnp.arange(PAGE_SIZE)
    col_r = jnp.arange(PAGE_SIZE * heads) // heads
    after = r[:, None] > col_r[None, :]
    return jnp.concatenate([after, jnp.ones((PAGE_SIZE, LANES), bool)], axis=1).astype(BF16)


def decode_attention(page_table, layer, tok_d, tok_f, slogf_t, cache_dk, cache_dv, cache_fk, cache_fv,
                     cache_lf, dsub_g, lam_vecs, lam_init):
    DB, n_pages = page_table.shape
    HD, HF = DIFF_HEADS, FOX_HEADS
    npp = max(n for n in (4, 2, 1) if n_pages % n == 0)
    steps = n_pages // npp

    def page(i, *tail):
        zeros = (0,) * (len(tail) + 1)
        return pl.BlockSpec((None, None, PAGE_SIZE) + tail,
                            lambda b, p, pt: (layer, pt[b, n_pages - 1 - (p * npp + i)]) + zeros)

    pages, page_args = [], []
    for i in range(npp):
        pages += [page(i, HD, DIFF_DV), page(i, HD, DIFF_DV), page(i, HF, FOX_DH), page(i, HF, FOX_DH), page(i, HF)]
        page_args += [cache_dk, cache_dv, cache_fk, cache_fv, cache_lf]
    ae = _suffix_matrix(HF)
    out_d = pl.BlockSpec((None, HD, DIFF_DV), lambda b, p, pt: (b, 0, 0))
    out_f = pl.BlockSpec((None, HF, FOX_DH), lambda b, p, pt: (b, 0, 0))
    grid_spec = pltpu.PrefetchScalarGridSpec(
        num_scalar_prefetch=1,
        grid=(DB, steps),
        in_specs=[
            pl.BlockSpec((None, 3, HD, DIFF_DV), lambda b, p, pt: (b, 0, 0, 0)),
            pl.BlockSpec((None, 3, HF, FOX_DH), lambda b, p, pt: (b, 0, 0, 0)),
            pl.BlockSpec((None, HF, 1), lambda b, p, pt: (b, 0, 0)),
            *pages,
            pl.BlockSpec(ae.shape, lambda b, p, pt: (0, 0)),
            pl.BlockSpec((1, DIFF_DV), lambda b, p, pt: (0, 0)),
            pl.BlockSpec((4, DIFF_DC), lambda b, p, pt: (0, 0)),
        ],
        out_specs=[out_d, out_f],
        scratch_shapes=[
            pltpu.VMEM((2 * HD, DIFF_DV), BF16), pltpu.VMEM((2 * HD, 1), F32), pltpu.VMEM((2 * HD, 1), F32),
            pltpu.VMEM((2 * HD, DIFF_DV), F32),
            pltpu.VMEM((HF, FOX_DH), BF16), pltpu.VMEM((HF, 1), F32), pltpu.VMEM((HF, 1), F32),
            pltpu.VMEM((HF, FOX_DH), F32),
            pltpu.VMEM((HF, 1), F32),
        ],
    )
    return pl.pallas_call(
        functools.partial(_decode_kernel, lam_init=lam_init, npp=npp),
        grid_spec=grid_spec,
        out_shape=[jax.ShapeDtypeStruct((DB, HD, DIFF_DV), F32), jax.ShapeDtypeStruct((DB, HF, FOX_DH), F32)],
        compiler_params=_cparams("parallel", "arbitrary"),
        name="decode_attention",
    )(page_table, tok_d, tok_f, slogf_t, *page_args, ae, dsub_g.reshape(1, DIFF_DV), jnp.stack(lam_vecs))


def _merge_kernel(od_ref, of_ref, wa_ref, wb_ref, ga_ref, gb_ref, o_ref):
    ya = jnp.dot(od_ref[...], wa_ref[...], preferred_element_type=F32)
    yb = jnp.dot(of_ref[...], wb_ref[...], preferred_element_type=F32)
    o_ref[...] = (ga_ref[...].astype(F32) * ya + gb_ref[...].astype(F32) * yb).astype(o_ref.dtype)


def merge(od, of, w_a, w_b, sga, sgb):
    T, Wd = od.shape
    Wf = of.shape[1]
    D = w_a.shape[1]
    tm = _tile(T, 1024)
    tn = _tile(D, 512)
    return pl.pallas_call(
        _merge_kernel,
        grid=(T // tm, D // tn),
        in_specs=[
            pl.BlockSpec((tm, Wd), lambda i, j: (i, 0)), pl.BlockSpec((tm, Wf), lambda i, j: (i, 0)),
            pl.BlockSpec((Wd, tn), lambda i, j: (0, j)), pl.BlockSpec((Wf, tn), lambda i, j: (0, j)),
            pl.BlockSpec((tm, tn), lambda i, j: (i, j)), pl.BlockSpec((tm, tn), lambda i, j: (i, j)),
        ],
        out_specs=pl.BlockSpec((tm, tn), lambda i, j: (i, j)),
        out_shape=jax.ShapeDtypeStruct((T, D), BF16),
        compiler_params=_cparams("parallel", "arbitrary"),
        name="merge",
    )(od, of, w_a, w_b, sga, sgb)


def _out_proj_kernel(mix_ref, w_ref, x_ref, o_ref):
    o_ref[...] = x_ref[...] + jnp.dot(mix_ref[...], w_ref[...], preferred_element_type=F32)


def out_proj_residual(mix, w_o, x):
    T, D = mix.shape
    N = w_o.shape[1]
    tm = _tile(T, 1024)
    tn = _tile(N, 512)
    return pl.pallas_call(
        _out_proj_kernel,
        grid=(T // tm, N // tn),
        in_specs=[
            pl.BlockSpec((tm, D), lambda i, j: (i, 0)), pl.BlockSpec((D, tn), lambda i, j: (0, j)),
            pl.BlockSpec((tm, tn), lambda i, j: (i, j)),
        ],
        out_specs=pl.BlockSpec((tm, tn), lambda i, j: (i, j)),
        out_shape=jax.ShapeDtypeStruct((T, N), F32),
        compiler_params=_cparams("parallel", "arbitrary"),
        name="out_proj_residual",
    )(mix, w_o, x)


def _peer_query_kernel(x_ref, g_ref, wq_ref, keys_ref, hb_ref, st_ref, hb_scr):
    j = pl.program_id(1)

    @pl.when(j == 0)
    def _():
        x = x_ref[...]
        ms = jnp.mean(x * x, axis=-1, keepdims=True)
        hb = (x * lax.rsqrt(ms + EPS) * g_ref[...]).astype(BF16)
        hb_scr[...] = hb
        hb_ref[...] = hb

    q = jnp.dot(hb_scr[...], wq_ref[...], preferred_element_type=F32).astype(BF16)
    for g in range(q.shape[1] // PEER_DHALF):
        st_ref[g] = lax.dot_general(keys_ref[g], q[:, g * PEER_DHALF:(g + 1) * PEER_DHALF],
                                    (((1,), (1,)), ((), ())), preferred_element_type=F32)


def peer_query(x, g, wq, keys):
    T, D = x.shape
    NQ = wq.shape[1]
    tm = _tile(T, 512)
    gpt = 4
    tn = gpt * PEER_DHALF
    return pl.pallas_call(
        _peer_query_kernel,
        grid=(T // tm, NQ // tn),
        in_specs=[
            pl.BlockSpec((tm, D), lambda i, j: (i, 0)), pl.BlockSpec((1, D), lambda i, j: (0, 0)),
            pl.BlockSpec((D, tn), lambda i, j: (0, j)),
            pl.BlockSpec((gpt, PEER_NKEYS, PEER_DHALF), lambda i, j: (j, 0, 0)),
        ],
        out_specs=[pl.BlockSpec((tm, D), lambda i, j: (i, 0)),
                   pl.BlockSpec((gpt, PEER_NKEYS, tm), lambda i, j: (j, 0, i))],
        out_shape=[jax.ShapeDtypeStruct((T, D), BF16),
                   jax.ShapeDtypeStruct((NQ // PEER_DHALF, PEER_NKEYS, T), F32)],
        scratch_shapes=[pltpu.VMEM((tm, D), BF16)],
        compiler_params=_cparams("parallel", "arbitrary"),
        name="peer_query",
    )(x, g.reshape(1, D), wq, keys)


def _topk_rows(xs, n):
    R, tt = xs[0].shape
    rows = lax.broadcasted_iota(jnp.int32, (R, tt), 0).astype(F32)
    slot = lax.broadcasted_iota(jnp.int32, (n, tt), 0)

    def body(r, carry):
        out = []
        for x, rank, vals in carry:
            m = jnp.max(x, axis=0, keepdims=True)
            idx = jnp.min(jnp.where(x == m, rows, float(R)), axis=0, keepdims=True)
            hit = rows == idx
            out.append((jnp.where(hit, NEG_INF, x), jnp.where(hit, lax.convert_element_type(r, F32), rank),
                        jnp.where(slot == r, m, vals)))
        return tuple(out)

    init = tuple((x, jnp.full((R, tt), float(n), F32), jnp.zeros((n, tt), F32)) for x in xs)
    res = lax.fori_loop(0, n, body, init)
    return [(rank, vals) for _, rank, vals in res]


def _peer_topk_kernel(s_ref, n0_ref, e0_ref, r1_ref, e1_ref):
    K = PEER_TOPK
    KB = K // 2

    def head(h, carry):
        s0 = s_ref[2 * h]
        s1 = s_ref[2 * h + 1]
        (rank0, v0), (rank1, v1) = _topk_rows((s0, s1), K)
        cand = jnp.concatenate([v0[0:1, :] + v1] + [v0[a:a + 1, :] + v1[0:KB, :] for a in range(1, K)], axis=0)
        row = lax.broadcasted_iota(jnp.int32, cand.shape, 0)
        ra = jnp.where(row < K, 0, (row - K) // KB + 1)
        rb = jnp.where(row < K, row, (row - K) % KB)
        cand = jnp.where((ra + 1) * (rb + 1) <= K, cand, NEG_INF)
        ((crank, cvals),) = _topk_rows((cand,), K)
        sel = crank < float(K)
        z = jnp.sum(jnp.where(sel, jnp.exp(cand - cvals[0:1, :]), 0.0), axis=0, keepdims=True)
        n0 = jnp.zeros_like(s0)
        for a in range(K):
            lo, hi = (0, K) if a == 0 else (K + KB * (a - 1), K + KB * a)
            cnt = jnp.sum(jnp.where(sel[lo:hi, :], 1.0, 0.0), axis=0, keepdims=True)
            n0 = jnp.where(rank0 == float(a), cnt, n0)
        n0_ref[h] = n0
        e0_ref[h] = jnp.exp(s0 - v0[0:1, :])
        r1_ref[h] = rank1.astype(r1_ref.dtype)
        e1_ref[h] = (jnp.exp(s1 - v1[0:1, :]) / z).astype(e1_ref.dtype)
        return carry

    lax.fori_loop(0, s_ref.shape[0] // 2, head, 0)


def peer_topk(scores_t):
    G, NK, T = scores_t.shape
    tt = _tile(T, LANES)
    spec = pl.BlockSpec((G // 2, NK, tt), lambda i: (0, 0, i))
    return pl.pallas_call(
        _peer_topk_kernel,
        grid=(T // tt,),
        in_specs=[pl.BlockSpec((G, NK, tt), lambda i: (0, 0, i))],
        out_specs=[spec] * 4,
        out_shape=[jax.ShapeDtypeStruct((G // 2, NK, T), dt) for dt in (F32, F32, BF16, BF16)],
        compiler_params=_cparams("parallel"),
        name="peer_topk",
    )(scores_t)


def _peer_mix_kernel(hb_ref, u_ref, v_ref, n0_ref, e0_ref, r1_ref, e1_ref, x_ref, o_ref):
    e = pl.program_id(1)

    @pl.when(e == 0)
    def _():
        o_ref[...] = x_ref[...]

    NK = PEER_NKEYS
    ipc = u_ref.shape[0] // NK
    at = lax.dot_general(u_ref[...], hb_ref[...], (((1,), (1,)), ((), ())), preferred_element_type=F32)
    act = (0.5 * at * (1.0 + lax.erf(at * (2.0 ** -0.5)))).astype(BF16)
    zero = jnp.zeros((), r1_ref.dtype)
    parts = []
    for ii in range(ipc):
        i = e * ipc + ii
        gate = None
        for h in range(n0_ref.shape[0]):
            n0row = n0_ref[h, pl.ds(i, 1), :].astype(r1_ref.dtype)
            e0row = e0_ref[h, pl.ds(i, 1), :].astype(e1_ref.dtype)
            term = jnp.where(r1_ref[h] < n0row, e1_ref[h], zero) * e0row
            gate = term if gate is None else gate + term
        parts.append(gate * act[ii * NK:(ii + 1) * NK, :])
    wt = jnp.concatenate(parts, axis=0)
    o_ref[...] += lax.dot_general(wt, v_ref[...], (((0,), (0,)), ((), ())), preferred_element_type=F32)


def peer_mix(hb, u_tab, v_tab, n0, e0, r1, e1, x):
    T, D = hb.shape
    NE = u_tab.shape[0]
    heads, NK, _ = n0.shape
    tm = _tile(T, 512)
    ec = 2 * NK
    once = dict(pipeline_mode=pl.Buffered(1))
    tok_in = functools.partial(pl.BlockSpec, (tm, D), lambda i, e: (i, 0))
    tab = pl.BlockSpec((ec, D), lambda i, e: (e, 0))
    desc = pl.BlockSpec((heads, NK, tm), lambda i, e: (0, 0, i), **once)
    return pl.pallas_call(
        _peer_mix_kernel,
        grid=(T // tm, NE // ec),
        in_specs=[tok_in(**once), tab, tab, desc, desc, desc, desc, tok_in(**once)],
        out_specs=pl.BlockSpec((tm, D), lambda i, e: (i, 0)),
        out_shape=jax.ShapeDtypeStruct((T, D), F32),
        compiler_params=_cparams("parallel", "arbitrary"),
        name="peer_mix",
    )(hb, u_tab, v_tab, n0, e0, r1, e1, x)


def _rope_tables(pos, n_rows):
    half = DIFF_DC // 2
    inv = ROPE_THETA ** (-jnp.arange(half, dtype=F32) / half)
    ang = pos.astype(F32)[:, None] * inv[None, :]
    cos = jnp.cos(ang)
    sin = jnp.sin(ang)
    cos2 = jnp.concatenate([cos, cos], axis=-1)
    sin2 = jnp.concatenate([-sin, sin], axis=-1)
    if cos2.shape[0] != n_rows:
        cos2 = jnp.broadcast_to(cos2, (n_rows, DIFF_DC))
        sin2 = jnp.broadcast_to(sin2, (n_rows, DIFF_DC))
    return cos2, sin2


def _token_front(x, pos_tables, w, l_params):
    cos, sin = pos_tables
    h = rmsnorm_cast(x, l_params["norm1_g"])
    dscale = DIFF_DC ** -0.5
    fscale = FOX_DH ** -0.5
    (dq,) = project(h, w["dq"], out_dtypes=(BF16,), gain=l_params["dq_g"], cos=cos, sin=sin, scale=dscale)
    dk32, dk16 = project(h, w["dk"], out_dtypes=(F32, BF16), gain=l_params["dk_g"], cos=cos, sin=sin)
    dv32, dv16 = project(h, w["dv"], out_dtypes=(F32, BF16))
    (fq,) = project(h, w["fq"], out_dtypes=(BF16,), gain=l_params["fq_g"], scale=fscale)
    fk32, fk16 = project(h, w["fk"], out_dtypes=(F32, BF16), gain=l_params["fk_g"])
    fv32, fv16 = project(h, w["fv"], out_dtypes=(F32, BF16))
    (sga,) = project(h, w["ga"], out_dtypes=(BF16,), act="sigmoid")
    (sgb,) = project(h, w["gb"], out_dtypes=(BF16,), act="sigmoid")
    return h, dict(dq=dq, dk32=dk32, dk16=dk16, dv32=dv32, dv16=dv16, fq=fq, fk32=fk32, fk16=fk16,
                   fv32=fv32, fv16=fv16, sga=sga, sgb=sgb)


def _token_back(x, od, of, t, w, l_params):
    mix = merge(od, of, w["w_a"], w["w_b"], t["sga"], t["sgb"])
    x2 = out_proj_residual(mix, w["w_o"], x)
    hb, scores_t = peer_query(x2, l_params["norm2_g"], w["peer_wq"], w["peer_keys"])
    n0, e0, r1, e1 = peer_topk(scores_t)
    return peer_mix(hb, w["peer_u"], w["peer_v"], n0, e0, r1, e1, x2)


def kernel(x_prompt, x_sample, cache_diff_k, cache_diff_v, cache_fox_k, cache_fox_v, cache_fox_logf, page_table, norm1_g, w_in, b_f, dq_g, dk_g, fq_g, fk_g, lam_q1, lam_k1, lam_q2, lam_k2, dsub_g, w_a, w_b, w_o, norm2_g, peer_wq, peer_keys, peer_u, peer_v):
    B, S, D = x_prompt.shape
    DB, DS, _ = x_sample.shape
    assert DS == 1
    depth = w_in.shape[0]
    n_pages = page_table.shape[1]
    past = n_pages * PAGE_SIZE
    WD = DIFF_HEADS * DIFF_DV
    WF = FOX_HEADS * FOX_DH
    sizes = (WD, WD, WD, WF, WF, WF, FOX_HEADS, D, D)
    names = ("dq", "dk", "dv", "fq", "fk", "fv", "fl", "ga", "gb")
    offs = [0]
    for n in sizes:
        offs.append(offs[-1] + n)

    pos_p = _rope_tables(jnp.arange(S, dtype=jnp.int32), S)
    pos_s = _rope_tables(past + jnp.arange(DS, dtype=jnp.int32), DB)

    xp = x_prompt.reshape(B * S, D)
    xs = x_sample.reshape(DB * DS, D)
    outs = [[] for _ in range(10)]
    for l in range(depth):
        lam_init = 0.8 - 0.6 * math.exp(-0.3 * l)
        w = {n: w_in[l, :, offs[i]:offs[i + 1]].astype(BF16) for i, n in enumerate(names)}
        w.update(w_a=w_a[l].astype(BF16), w_b=w_b[l].astype(BF16), w_o=w_o[l].astype(BF16),
                 peer_wq=peer_wq[l].astype(BF16),
                 peer_keys=peer_keys[l].reshape(PEER_HEADS * 2, PEER_NKEYS, PEER_DHALF).astype(BF16),
                 peer_u=peer_u[l].astype(BF16), peer_v=peer_v[l].astype(BF16))
        lp = dict(norm1_g=norm1_g[l], dq_g=dq_g[l], dk_g=dk_g[l], fq_g=fq_g[l], fk_g=fk_g[l], norm2_g=norm2_g[l])
        lam_vecs = (lam_q1[l], lam_k1[l], lam_q2[l], lam_k2[l])

        h, t = _token_front(xp, pos_p, w, lp)
        logf, ct, ck = forget_gate(h, w["fl"], b_f[l], B, S)
        od = diff_attention(t["dq"], t["dk16"], t["dv16"], dsub_g[l], lam_vecs, lam_init, B, S)
        of = fox_attention(t["fq"], t["fk16"], t["fv16"], ct, ck, B, S)
        xp = _token_back(xp, od, of, t, w, lp)
        outs[0].append(t["dk32"].reshape(B, S, DIFF_HEADS, DIFF_DV))
        outs[1].append(t["dv32"].reshape(B, S, DIFF_HEADS, DIFF_DV))
        outs[2].append(t["fk32"].reshape(B, S, FOX_HEADS, FOX_DH))
        outs[3].append(t["fv32"].reshape(B, S, FOX_HEADS, FOX_DH))
        outs[4].append(logf)

        hs, ts_ = _token_front(xs, pos_s, w, lp)
        slogf, _, _ = forget_gate(hs, w["fl"], b_f[l], 1, DB)
        slogf = slogf.reshape(DB, DS, FOX_HEADS)
        tok_d = jnp.stack([ts_[n].astype(F32).reshape(DB, DIFF_HEADS, DIFF_DV) for n in ("dq", "dk16", "dv16")], axis=1)
        tok_f = jnp.stack([ts_[n].astype(F32).reshape(DB, FOX_HEADS, FOX_DH) for n in ("fq", "fk16", "fv16")], axis=1)
        sod, sof = decode_attention(
            page_table, l, tok_d, tok_f, slogf.reshape(DB, FOX_HEADS, 1),
            cache_diff_k, cache_diff_v, cache_fox_k, cache_fox_v, cache_fox_logf,
            dsub_g[l], lam_vecs, lam_init)
        sod = sod.astype(BF16)
        sof = sof.astype(BF16)
        xs = _token_back(xs, sod.reshape(DB, WD), sof.reshape(DB, WF), ts_, w, lp)
        outs[5].append(ts_["dk32"].reshape(DB, DS, DIFF_HEADS, DIFF_DV))
        outs[6].append(ts_["dv32"].reshape(DB, DS, DIFF_HEADS, DIFF_DV))
        outs[7].append(ts_["fk32"].reshape(DB, DS, FOX_HEADS, FOX_DH))
        outs[8].append(ts_["fv32"].reshape(DB, DS, FOX_HEADS, FOX_DH))
        outs[9].append(slogf)

    return (xp.reshape(B, S, D), xs.reshape(DB, DS, D)) + tuple(jnp.stack(o) for o in outs)
```

```python
import functools
import math

import jax
import jax.numpy as jnp
from jax import lax
from jax.experimental import pallas as pl
from jax.experimental.pallas import tpu as pltpu

DIFF_HEADS = 8
DIFF_DC = 128
DIFF_DV = 2 * DIFF_DC
FOX_HEADS = 16
FOX_DH = 128
PEER_HEADS = 8
PEER_NKEYS = 128
PEER_DHALF = 128
PEER_TOPK = 16
PAGE_SIZE = 128
ROPE_THETA = 10000.0
EPS = 1e-6
LANES = 128
VMEM_LIMIT = 56 * 1024 * 1024

F32 = jnp.float32
BF16 = jnp.bfloat16
NEG_INF = float("-inf")
_NT = (((1,), (1,)), ((), ()))
_TN = (((0,), (0,)), ((), ()))


def _cparams(*sem):
    return pltpu.CompilerParams(dimension_semantics=sem, vmem_limit_bytes=VMEM_LIMIT)


def _tile(n, pref):
    t = min(n, pref)
    while n % t:
        t //= 2
    return t


def _split3(x):
    hi = x.astype(BF16)
    r1 = x - hi.astype(F32)
    mid = r1.astype(BF16)
    lo = (r1 - mid.astype(F32)).astype(BF16)
    return hi, mid, lo


def _log_sigmoid(x):
    return jnp.minimum(x, 0.0) - jnp.log1p(jnp.exp(-jnp.abs(x)))


def _sigmoid(x):
    return 1.0 / (1.0 + jnp.exp(-x))


def _rmsnorm_kernel(x_ref, g_ref, o_ref):
    x = x_ref[...]
    ms = jnp.mean(x * x, axis=-1, keepdims=True)
    o_ref[...] = (x * lax.rsqrt(ms + EPS) * g_ref[...]).astype(o_ref.dtype)


def rmsnorm_cast(x, g):
    T, D = x.shape
    tm = _tile(T, 256)
    return pl.pallas_call(
        _rmsnorm_kernel,
        grid=(T // tm,),
        in_specs=[pl.BlockSpec((tm, D), lambda i: (i, 0)), pl.BlockSpec((1, D), lambda i: (0, 0))],
        out_specs=pl.BlockSpec((tm, D), lambda i: (i, 0)),
        out_shape=jax.ShapeDtypeStruct((T, D), BF16),
        compiler_params=_cparams("parallel"),
        name="rmsnorm_cast",
    )(x, g.reshape(1, D))


def _proj_kernel(*refs, norm, rope, act, scale):
    h_ref, w_ref = refs[0], refs[1]
    pos = 2
    gain_ref = cos_ref = sin_ref = None
    if norm:
        gain_ref = refs[pos]
        pos += 1
    if rope:
        cos_ref, sin_ref = refs[pos], refs[pos + 1]
        pos += 2
    out_refs = refs[pos:]
    z = jnp.dot(h_ref[...], w_ref[...].astype(BF16), preferred_element_type=F32)
    tn = z.shape[1]
    if norm:
        for g in range(tn // LANES):
            x = z[:, g * LANES:(g + 1) * LANES]
            ms = jnp.mean(x * x, axis=-1, keepdims=True)
            y = x * lax.rsqrt(ms + EPS) * gain_ref[...]
            if rope:
                y = y * cos_ref[...] + pltpu.roll(y, LANES // 2, 1) * sin_ref[...]
            if scale != 1.0:
                y = y * scale
            for o_ref in out_refs:
                o_ref[:, g * LANES:(g + 1) * LANES] = y.astype(o_ref.dtype)
    else:
        if act == "sigmoid":
            z = _sigmoid(z)
        for o_ref in out_refs:
            o_ref[...] = z.astype(o_ref.dtype)


def project(h, w, cols, *, out_dtypes, gain=None, cos=None, sin=None, act=None, scale=1.0, tm_pref=1024, tn_pref=512):
    T, D = h.shape
    off, N = cols
    tm = _tile(T, tm_pref)
    tn = _tile(N, tn_pref)
    assert off % tn == 0
    norm = gain is not None
    rope = cos is not None
    in_specs = [pl.BlockSpec((tm, D), lambda i, j: (i, 0)), pl.BlockSpec((D, tn), lambda i, j: (0, off // tn + j))]
    args = [h, w]
    if norm:
        in_specs.append(pl.BlockSpec((1, LANES), lambda i, j: (0, 0)))
        args.append(gain.reshape(1, LANES))
    if rope:
        nrep = cos.shape[0] // tm
        in_specs += [pl.BlockSpec((tm, LANES), lambda i, j: (i % nrep, 0))] * 2
        args += [cos, sin]
    outs = pl.pallas_call(
        functools.partial(_proj_kernel, norm=norm, rope=rope, act=act, scale=scale),
        grid=(T // tm, N // tn),
        in_specs=in_specs,
        out_specs=[pl.BlockSpec((tm, tn), lambda i, j: (i, j)) for _ in out_dtypes],
        out_shape=[jax.ShapeDtypeStruct((T, N), dt) for dt in out_dtypes],
        compiler_params=_cparams("parallel", "arbitrary"),
        name="project",
    )(*args)
    return outs


def _forget_kernel(h_ref, w_ref, wt_ref, b_ref, bt_ref, logf_ref, ct_ref, ck_ref, carry, carry_t):
    s = pl.program_id(1)

    @pl.when(s == 0)
    def _():
        carry[...] = jnp.zeros_like(carry)
        carry_t[...] = jnp.zeros_like(carry_t)

    h = h_ref[...]
    ts = h.shape[0]
    fl = jnp.dot(h, w_ref[...].astype(BF16), preferred_element_type=F32)
    logf = _log_sigmoid(fl + b_ref[...])
    logf_ref[...] = logf
    row = lax.broadcasted_iota(jnp.int32, (ts, ts), 0)
    col = lax.broadcasted_iota(jnp.int32, (ts, ts), 1)
    lower = (row >= col).astype(BF16)
    cum = sum(jnp.dot(lower, p, preferred_element_type=F32) for p in _split3(logf))
    c = cum + carry[...]
    carry[...] = c[ts - 1:ts, :]
    for hd in range(c.shape[1]):
        ck_ref[hd] = jnp.broadcast_to(c[:, hd:hd + 1], (ts, LANES))
    flt = lax.dot_general(wt_ref[...].astype(BF16), h, _NT, preferred_element_type=F32)
    logft = _log_sigmoid(flt + bt_ref[...])
    upper = (row <= col).astype(BF16)
    cumt = sum(jnp.dot(p, upper, preferred_element_type=F32) for p in _split3(logft))
    ct = cumt + carry_t[...]
    ct_ref[...] = ct
    carry_t[...] = ct[:, ts - 1:ts]


def forget_gate(h, w_fl, b_f, B, S):
    D = h.shape[1]
    H = w_fl.shape[1]
    ts = _tile(S, 512)
    ns = S // ts
    return pl.pallas_call(
        _forget_kernel,
        grid=(B, ns),
        in_specs=[
            pl.BlockSpec((ts, D), lambda b, s: (b * ns + s, 0)),
            pl.BlockSpec((D, H), lambda b, s: (0, 0)),
            pl.BlockSpec((H, D), lambda b, s: (0, 0)),
            pl.BlockSpec((1, H), lambda b, s: (0, 0)),
            pl.BlockSpec((H, 1), lambda b, s: (0, 0)),
        ],
        out_specs=[
            pl.BlockSpec((None, ts, H), lambda b, s: (b, s, 0)),
            pl.BlockSpec((None, H, ts), lambda b, s: (b, 0, s)),
            pl.BlockSpec((None, H, ts, LANES), lambda b, s: (b, 0, s, 0)),
        ],
        out_shape=[
            jax.ShapeDtypeStruct((B, S, H), F32),
            jax.ShapeDtypeStruct((B, H, S), F32),
            jax.ShapeDtypeStruct((B, H, S, LANES), F32),
        ],
        scratch_shapes=[pltpu.VMEM((1, H), F32), pltpu.VMEM((H, 1), F32)],
        compiler_params=_cparams("parallel", "arbitrary"),
        name="forget_gate",
    )(h, w_fl, w_fl.T, b_f.reshape(1, H), b_f.reshape(H, 1))


def _lambda_value(lq1, lk1, lq2, lk2, lam_init):
    return (jnp.exp(jnp.sum(lq1 * lk1, axis=-1, keepdims=True))
            - jnp.exp(jnp.sum(lq2 * lk2, axis=-1, keepdims=True)) + lam_init)


def _softmax_step(s, v, m, l, acc):
    m_new = jnp.maximum(m, jnp.max(s, axis=0, keepdims=True))
    alpha = jnp.exp(m - m_new)
    p = jnp.exp(s - m_new)
    l = alpha * l + jnp.sum(p, axis=0, keepdims=True)
    acc = alpha * acc + lax.dot_general(v, p.astype(BF16), _TN, preferred_element_type=F32)
    return m_new, l, acc


def _causal_mask(s, key_offset):
    krow = lax.broadcasted_iota(jnp.int32, s.shape, 0)
    qcol = lax.broadcasted_iota(jnp.int32, s.shape, 1)
    return jnp.where(krow + key_offset <= qcol, s, NEG_INF)


def _causal_sweep(chunk, init, qi, tq, tk):
    n_full = qi * (tq // tk)
    carry = lax.fori_loop(0, n_full, lambda j, cr: chunk(j, cr, None), init)
    for d in range(tq // tk):
        carry = chunk(n_full + d, carry, d * tk)
    return carry


ATTN_TQ = 1024
ATTN_TK = 1024


def _diff_attn_kernel(q_ref, k_ref, v_ref, g_ref, lq1_ref, lk1_ref, lq2_ref, lk2_ref, o_ref, *, lam_init, tk):
    qi = pl.program_id(2)
    tq = q_ref.shape[0]
    q = q_ref[...]

    def chunk(j, carry, key_offset):
        rows = pl.ds(pl.multiple_of(j * tk, tk), tk)
        k = k_ref[rows, :]
        v = v_ref[rows, :]
        out = []
        for c in range(2):
            s = lax.dot_general(k[:, c * DIFF_DC:(c + 1) * DIFF_DC], q[:, c * DIFF_DC:(c + 1) * DIFF_DC], _NT,
                                preferred_element_type=F32)
            if key_offset is not None:
                s = _causal_mask(s, key_offset)
            out.append(_softmax_step(s, v, *carry[c]))
        return tuple(out)

    init = tuple((jnp.full((1, tq), NEG_INF, F32), jnp.zeros((1, tq), F32), jnp.zeros((DIFF_DV, tq), F32))
                 for _ in range(2))
    (_, l1, a1), (_, l2, a2) = _causal_sweep(chunk, init, qi, tq, tk)
    lam = _lambda_value(lq1_ref[...], lk1_ref[...], lq2_ref[...], lk2_ref[...], lam_init)
    o = a1 / l1 - lam * (a2 / l2)
    ms = jnp.mean(o * o, axis=0, keepdims=True)
    o_ref[...] = ((o * lax.rsqrt(ms + EPS)).T * g_ref[...] * (1.0 - lam_init)).astype(o_ref.dtype)


def diff_attention(q, k, v, dsub_g, lam_vecs, lam_init, B, S):
    tq = _tile(S, ATTN_TQ)
    tk = _tile(tq, ATTN_TK)
    n = S // tq
    W = DIFF_DV
    qspec = pl.BlockSpec((tq, W), lambda b, h, qi: (b * n + qi, h))
    kspec = pl.BlockSpec((S, W), lambda b, h, qi: (b, h))
    vec = pl.BlockSpec((1, DIFF_DC), lambda b, h, qi: (0, 0))
    return pl.pallas_call(
        functools.partial(_diff_attn_kernel, lam_init=lam_init, tk=tk),
        grid=(B, DIFF_HEADS, n),
        in_specs=[qspec, kspec, kspec, pl.BlockSpec((1, W), lambda b, h, qi: (0, 0)), vec, vec, vec, vec],
        out_specs=qspec,
        out_shape=jax.ShapeDtypeStruct((B * S, DIFF_HEADS * W), BF16),
        compiler_params=_cparams("parallel", "parallel", "arbitrary"),
        name="diff_attention",
    )(q, k, v, dsub_g.reshape(1, W), *[x.reshape(1, DIFF_DC) for x in lam_vecs])


def _fox_attn_kernel(q_ref, k_ref, v_ref, ct_ref, ck_ref, o_ref, *, tk):
    h = pl.program_id(1)
    qi = pl.program_id(2)
    tq = q_ref.shape[0]
    q = q_ref[...]
    cq = ct_ref[pl.ds(h, 1), :]

    def chunk(j, carry, key_offset):
        rows = pl.ds(pl.multiple_of(j * tk, tk), tk)
        s = lax.dot_general(k_ref[rows, :], q, _NT, preferred_element_type=F32)
        ck = ck_ref[rows, :]
        s = jnp.concatenate([s[:, i * LANES:(i + 1) * LANES] + (cq[:, i * LANES:(i + 1) * LANES] - ck)
                             for i in range(tq // LANES)], axis=1)
        if key_offset is not None:
            s = _causal_mask(s, key_offset)
        return _softmax_step(s, v_ref[rows, :], *carry)

    init = (jnp.full((1, tq), NEG_INF, F32), jnp.zeros((1, tq), F32), jnp.zeros((FOX_DH, tq), F32))
    _, l, acc = _causal_sweep(chunk, init, qi, tq, tk)
    o_ref[...] = (acc / l).T.astype(o_ref.dtype)


def fox_attention(q, k, v, ct, ck, B, S):
    tq = _tile(S, ATTN_TQ)
    tk = _tile(tq, ATTN_TK)
    n = S // tq
    H = FOX_HEADS
    qspec = pl.BlockSpec((tq, FOX_DH), lambda b, h, qi: (b * n + qi, h))
    kspec = pl.BlockSpec((S, FOX_DH), lambda b, h, qi: (b, h))
    return pl.pallas_call(
        functools.partial(_fox_attn_kernel, tk=tk),
        grid=(B, H, n),
        in_specs=[
            qspec, kspec, kspec,
            pl.BlockSpec((None, H, tq), lambda b, h, qi: (b, 0, qi)),
            pl.BlockSpec((None, None, S, LANES), lambda b, h, qi: (b, h, 0, 0)),
        ],
        out_specs=qspec,
        out_shape=jax.ShapeDtypeStruct((B * S, H * FOX_DH), BF16),
        compiler_params=_cparams("parallel", "parallel", "arbitrary"),
        name="fox_attention",
    )(q, k, v, ct, ck)


def _decode_kernel(*refs, lam_init, npp):
    pt_ref, td_ref, tf_ref, slogf_ref = refs[:4]
    pages = refs[4:4 + 5 * npp]
    ae_ref, g_ref, lamv_ref, od_ref, of_ref = refs[4 + 5 * npp:9 + 5 * npp]
    qd_rows, md, ld, accd, qf_rows, mf, lf, accf, carry = refs[9 + 5 * npp:]
    del pt_ref
    p = pl.program_id(1)
    HD = DIFF_HEADS
    HF = FOX_HEADS

    @pl.when(p == 0)
    def _():
        q8 = td_ref[0]
        lane = lax.broadcasted_iota(jnp.int32, q8.shape, 1)
        qr = jnp.concatenate([jnp.where(lane < DIFF_DC, q8, 0.0), jnp.where(lane >= DIFF_DC, q8, 0.0)], axis=0)
        qd_rows[...] = qr.astype(BF16)
        md[...] = jnp.sum(qr * jnp.concatenate([td_ref[1], td_ref[1]], axis=0), axis=-1, keepdims=True)
        ld[...] = jnp.ones_like(ld)
        accd[...] = jnp.concatenate([td_ref[2], td_ref[2]], axis=0)
        qf = tf_ref[0]
        qf_rows[...] = qf.astype(BF16)
        mf[...] = jnp.sum(qf * tf_ref[1], axis=-1, keepdims=True)
        lf[...] = jnp.ones_like(lf)
        accf[...] = tf_ref[2]
        carry[...] = jnp.zeros_like(carry)

    def update(scores, heads, values, m, l, acc):
        row = lax.broadcasted_iota(jnp.int32, scores[0].shape, 0)
        col = lax.broadcasted_iota(jnp.int32, scores[0].shape, 1)
        own = col % heads == row % heads
        scores = [jnp.where(own, s, NEG_INF) for s in scores]
        m_prev = m[...]
        m_new = m_prev
        for s in scores:
            m_new = jnp.maximum(m_new, jnp.max(s, axis=-1, keepdims=True))
        alpha = jnp.exp(m_prev - m_new)
        l_new = alpha * l[...]
        acc_new = alpha * acc[...]
        for s, v2 in zip(scores, values):
            pr = jnp.exp(s - m_new)
            l_new = l_new + jnp.sum(pr, axis=-1, keepdims=True)
            acc_new = acc_new + jnp.dot(pr.astype(BF16), v2, preferred_element_type=F32)
        l[...] = l_new
        acc[...] = acc_new
        m[...] = m_new

    nd = PAGE_SIZE * HD
    nf = PAGE_SIZE * HF
    sd, vd, sf, vf = [], [], [], []
    total = carry[...]
    for i in range(npp):
        pdk_ref, pdv_ref, pfk_ref, pfv_ref, plf_ref = pages[5 * i:5 * i + 5]
        kd2 = pdk_ref[...].reshape(nd, DIFF_DV).astype(BF16)
        sd.append(lax.dot_general(qd_rows[...], kd2, _NT, preferred_element_type=F32))
        vd.append(pdv_ref[...].reshape(nd, DIFF_DV).astype(BF16))
        b2 = sum(lax.dot_general(piece, ae_ref[...], _TN, preferred_element_type=F32)
                 for piece in _split3(plf_ref[...]))
        bias = slogf_ref[...] + total + b2[:, :nf]
        total = total + b2[:, nf:nf + 1]
        kf2 = pfk_ref[...].reshape(nf, FOX_DH).astype(BF16)
        sf.append(lax.dot_general(qf_rows[...], kf2, _NT, preferred_element_type=F32) + bias)
        vf.append(pfv_ref[...].reshape(nf, FOX_DH).astype(BF16))
    carry[...] = total
    update(sd, HD, vd, md, ld, accd)
    update(sf, HF, vf, mf, lf, accf)

    @pl.when(p == pl.num_programs(1) - 1)
    def _():
        lam = _lambda_value(lamv_ref[0:1, :], lamv_ref[1:2, :], lamv_ref[2:3, :], lamv_ref[3:4, :], lam_init)
        od = accd[...] / ld[...]
        o = od[:HD] - lam * od[HD:]
        ms = jnp.mean(o * o, axis=-1, keepdims=True)
        od_ref[...] = o * lax.rsqrt(ms + EPS) * g_ref[...] * (1.0 - lam_init)
        of_ref[...] = accf[...] / lf[...]


def _suffix_matrix(heads):
    r = jnp.arange(PAGE_SIZE)
    col_r = jnp.arange(PAGE_SIZE * heads) // heads
    after = r[:, None] > col_r[None, :]
    return jnp.concatenate([after, jnp.ones((PAGE_SIZE, LANES), bool)], axis=1).astype(BF16)


def decode_attention(page_table, layer, tok_d, tok_f, slogf_t, cache_dk, cache_dv, cache_fk, cache_fv,
                     cache_lf, dsub_g, lam_vecs, lam_init):
    DB, n_pages = page_table.shape
    HD, HF = DIFF_HEADS, FOX_HEADS
    npp = max(n for n in (4, 2, 1) if n_pages % n == 0)
    steps = n_pages // npp

    def page(i, *tail):
        zeros = (0,) * (len(tail) + 1)
        return pl.BlockSpec((None, None, PAGE_SIZE) + tail,
                            lambda b, p, pt: (layer, pt[b, n_pages - 1 - (p * npp + i)]) + zeros)

    pages, page_args = [], []
    for i in range(npp):
        pages += [page(i, HD, DIFF_DV), page(i, HD, DIFF_DV), page(i, HF, FOX_DH), page(i, HF, FOX_DH), page(i, HF)]
        page_args += [cache_dk, cache_dv, cache_fk, cache_fv, cache_lf]
    ae = _suffix_matrix(HF)
    out_d = pl.BlockSpec((None, HD, DIFF_DV), lambda b, p, pt: (b, 0, 0))
    out_f = pl.BlockSpec((None, HF, FOX_DH), lambda b, p, pt: (b, 0, 0))
    grid_spec = pltpu.PrefetchScalarGridSpec(
        num_scalar_prefetch=1,
        grid=(DB, steps),
        in_specs=[
            pl.BlockSpec((None, 3, HD, DIFF_DV), lambda b, p, pt: (b, 0, 0, 0)),
            pl.BlockSpec((None, 3, HF, FOX_DH), lambda b, p, pt: (b, 0, 0, 0)),
            pl.BlockSpec((None, HF, 1), lambda b, p, pt: (b, 0, 0)),
            *pages,
            pl.BlockSpec(ae.shape, lambda b, p, pt: (0, 0)),
            pl.BlockSpec((1, DIFF_DV), lambda b, p, pt: (0, 0)),
            pl.BlockSpec((4, DIFF_DC), lambda b, p, pt: (0, 0)),
        ],
        out_specs=[out_d, out_f],
        scratch_shapes=[
            pltpu.VMEM((2 * HD, DIFF_DV), BF16), pltpu.VMEM((2 * HD, 1), F32), pltpu.VMEM((2 * HD, 1), F32),
            pltpu.VMEM((2 * HD, DIFF_DV), F32),
            pltpu.VMEM((HF, FOX_DH), BF16), pltpu.VMEM((HF, 1), F32), pltpu.VMEM((HF, 1), F32),
            pltpu.VMEM((HF, FOX_DH), F32),
            pltpu.VMEM((HF, 1), F32),
        ],
    )
    return pl.pallas_call(
        functools.partial(_decode_kernel, lam_init=lam_init, npp=npp),
        grid_spec=grid_spec,
        out_shape=[jax.ShapeDtypeStruct((DB, HD, DIFF_DV), F32), jax.ShapeDtypeStruct((DB, HF, FOX_DH), F32)],
        compiler_params=_cparams("parallel", "arbitrary"),
        name="decode_attention",
    )(page_table, tok_d, tok_f, slogf_t, *page_args, ae, dsub_g.reshape(1, DIFF_DV), jnp.stack(lam_vecs))


def _merge_kernel(od_ref, of_ref, wa_ref, wb_ref, ga_ref, gb_ref, o_ref):
    ya = jnp.dot(od_ref[...], wa_ref[...].astype(BF16), preferred_element_type=F32)
    yb = jnp.dot(of_ref[...], wb_ref[...].astype(BF16), preferred_element_type=F32)
    o_ref[...] = (ga_ref[...].astype(F32) * ya + gb_ref[...].astype(F32) * yb).astype(o_ref.dtype)


def merge(od, of, w_a, w_b, sga, sgb):
    T, Wd = od.shape
    Wf = of.shape[1]
    D = w_a.shape[1]
    tm = _tile(T, 1024)
    tn = _tile(D, 512)
    return pl.pallas_call(
        _merge_kernel,
        grid=(T // tm, D // tn),
        in_specs=[
            pl.BlockSpec((tm, Wd), lambda i, j: (i, 0)), pl.BlockSpec((tm, Wf), lambda i, j: (i, 0)),
            pl.BlockSpec((Wd, tn), lambda i, j: (0, j)), pl.BlockSpec((Wf, tn), lambda i, j: (0, j)),
            pl.BlockSpec((tm, tn), lambda i, j: (i, j)), pl.BlockSpec((tm, tn), lambda i, j: (i, j)),
        ],
        out_specs=pl.BlockSpec((tm, tn), lambda i, j: (i, j)),
        out_shape=jax.ShapeDtypeStruct((T, D), BF16),
        compiler_params=_cparams("parallel", "arbitrary"),
        name="merge",
    )(od, of, w_a, w_b, sga, sgb)


def _out_proj_kernel(mix_ref, w_ref, x_ref, o_ref):
    o_ref[...] = x_ref[...] + jnp.dot(mix_ref[...], w_ref[...].astype(BF16), preferred_element_type=F32)


def out_proj_residual(mix, w_o, x):
    T, D = mix.shape
    N = w_o.shape[1]
    tm = _tile(T, 1024)
    tn = _tile(N, 512)
    return pl.pallas_call(
        _out_proj_kernel,
        grid=(T // tm, N // tn),
        in_specs=[
            pl.BlockSpec((tm, D), lambda i, j: (i, 0)), pl.BlockSpec((D, tn), lambda i, j: (0, j)),
            pl.BlockSpec((tm, tn), lambda i, j: (i, j)),
        ],
        out_specs=pl.BlockSpec((tm, tn), lambda i, j: (i, j)),
        out_shape=jax.ShapeDtypeStruct((T, N), F32),
        compiler_params=_cparams("parallel", "arbitrary"),
        name="out_proj_residual",
    )(mix, w_o, x)


def _peer_query_kernel(x_ref, g_ref, wq_ref, keys_ref, hb_ref, st_ref, hb_scr):
    j = pl.program_id(1)

    @pl.when(j == 0)
    def _():
        x = x_ref[...]
        ms = jnp.mean(x * x, axis=-1, keepdims=True)
        hb = (x * lax.rsqrt(ms + EPS) * g_ref[...]).astype(BF16)
        hb_scr[...] = hb
        hb_ref[...] = hb

    q = jnp.dot(hb_scr[...], wq_ref[...].astype(BF16), preferred_element_type=F32).astype(BF16)
    for g in range(q.shape[1] // PEER_DHALF):
        st_ref[g] = lax.dot_general(keys_ref[g], q[:, g * PEER_DHALF:(g + 1) * PEER_DHALF],
                                    (((1,), (1,)), ((), ())), preferred_element_type=F32)


def peer_query(x, g, wq, keys):
    T, D = x.shape
    NQ = wq.shape[1]
    tm = _tile(T, 512)
    gpt = 4
    tn = gpt * PEER_DHALF
    return pl.pallas_call(
        _peer_query_kernel,
        grid=(T // tm, NQ // tn),
        in_specs=[
            pl.BlockSpec((tm, D), lambda i, j: (i, 0)), pl.BlockSpec((1, D), lambda i, j: (0, 0)),
            pl.BlockSpec((D, tn), lambda i, j: (0, j)),
            pl.BlockSpec((gpt, PEER_NKEYS, PEER_DHALF), lambda i, j: (j, 0, 0)),
        ],
        out_specs=[pl.BlockSpec((tm, D), lambda i, j: (i, 0)),
                   pl.BlockSpec((gpt, PEER_NKEYS, tm), lambda i, j: (j, 0, i))],
        out_shape=[jax.ShapeDtypeStruct((T, D), BF16),
                   jax.ShapeDtypeStruct((NQ // PEER_DHALF, PEER_NKEYS, T), F32)],
        scratch_shapes=[pltpu.VMEM((tm, D), BF16)],
        compiler_params=_cparams("parallel", "arbitrary"),
        name="peer_query",
    )(x, g.reshape(1, D), wq, keys)


def _topk_rows(xs, n):
    R, tt = xs[0].shape
    rows = lax.broadcasted_iota(jnp.int32, (R, tt), 0).astype(F32)
    slot = lax.broadcasted_iota(jnp.int32, (n, tt), 0)

    def body(r, carry):
        out = []
        for x, rank, vals in carry:
            m = jnp.max(x, axis=0, keepdims=True)
            idx = jnp.min(jnp.where(x == m, rows, float(R)), axis=0, keepdims=True)
            hit = rows == idx
            out.append((jnp.where(hit, NEG_INF, x), jnp.where(hit, lax.convert_element_type(r, F32), rank),
                        jnp.where(slot == r, m, vals)))
        return tuple(out)

    init = tuple((x, jnp.full((R, tt), float(n), F32), jnp.zeros((n, tt), F32)) for x in xs)
    res = lax.fori_loop(0, n, body, init)
    return [(rank, vals) for _, rank, vals in res]


def _peer_topk_kernel(s_ref, n0_ref, e0_ref, r1_ref, e1_ref):
    K = PEER_TOPK
    KB = K // 2

    def head(h, carry):
        s0 = s_ref[2 * h]
        s1 = s_ref[2 * h + 1]
        (rank0, v0), (rank1, v1) = _topk_rows((s0, s1), K)
        cand = jnp.concatenate([v0[0:1, :] + v1] + [v0[a:a + 1, :] + v1[0:KB, :] for a in range(1, K)], axis=0)
        row = lax.broadcasted_iota(jnp.int32, cand.shape, 0)
        ra = jnp.where(row < K, 0, (row - K) // KB + 1)
        rb = jnp.where(row < K, row, (row - K) % KB)
        cand = jnp.where((ra + 1) * (rb + 1) <= K, cand, NEG_INF)
        ((crank, cvals),) = _topk_rows((cand,), K)
        sel = crank < float(K)
        z = jnp.sum(jnp.where(sel, jnp.exp(cand - cvals[0:1, :]), 0.0), axis=0, keepdims=True)
        n0 = jnp.zeros_like(s0)
        for a in range(K):
            lo, hi = (0, K) if a == 0 else (K + KB * (a - 1), K + KB * a)
            cnt = jnp.sum(jnp.where(sel[lo:hi, :], 1.0, 0.0), axis=0, keepdims=True)
            n0 = jnp.where(rank0 == float(a), cnt, n0)
        n0_ref[h] = n0
        e0_ref[h] = jnp.exp(s0 - v0[0:1, :])
        r1_ref[h] = rank1.astype(r1_ref.dtype)
        e1_ref[h] = (jnp.exp(s1 - v1[0:1, :]) / z).astype(e1_ref.dtype)
        return carry

    lax.fori_loop(0, s_ref.shape[0] // 2, head, 0)


def peer_topk(scores_t):
    G, NK, T = scores_t.shape
    tt = _tile(T, LANES)
    spec = pl.BlockSpec((G // 2, NK, tt), lambda i: (0, 0, i))
    return pl.pallas_call(
        _peer_topk_kernel,
        grid=(T // tt,),
        in_specs=[pl.BlockSpec((G, NK, tt), lambda i: (0, 0, i))],
        out_specs=[spec] * 4,
        out_shape=[jax.ShapeDtypeStruct((G // 2, NK, T), dt) for dt in (F32, F32, BF16, BF16)],
        compiler_params=_cparams("parallel"),
        name="peer_topk",
    )(scores_t)


def _peer_mix_kernel(hb_ref, u_ref, v_ref, n0_ref, e0_ref, r1_ref, e1_ref, x_ref, o_ref):
    e = pl.program_id(1)

    @pl.when(e == 0)
    def _():
        o_ref[...] = x_ref[...]

    NK = PEER_NKEYS
    ipc = u_ref.shape[0] // NK
    at = lax.dot_general(u_ref[...], hb_ref[...], (((1,), (1,)), ((), ())), preferred_element_type=F32)
    act = (0.5 * at * (1.0 + lax.erf(at * (2.0 ** -0.5)))).astype(BF16)
    zero = jnp.zeros((), r1_ref.dtype)
    parts = []
    for ii in range(ipc):
        i = e * ipc + ii
        gate = None
        for h in range(n0_ref.shape[0]):
            n0row = n0_ref[h, pl.ds(i, 1), :].astype(r1_ref.dtype)
            e0row = e0_ref[h, pl.ds(i, 1), :].astype(e1_ref.dtype)
            term = jnp.where(r1_ref[h] < n0row, e1_ref[h], zero) * e0row
            gate = term if gate is None else gate + term
        parts.append(gate * act[ii * NK:(ii + 1) * NK, :])
    wt = jnp.concatenate(parts, axis=0)
    o_ref[...] += lax.dot_general(wt, v_ref[...], (((0,), (0,)), ((), ())), preferred_element_type=F32)


def peer_mix(hb, u_tab, v_tab, n0, e0, r1, e1, x):
    T, D = hb.shape
    NE = u_tab.shape[0]
    heads, NK, _ = n0.shape
    tm = _tile(T, 512)
    ec = 4 * NK
    once = dict(pipeline_mode=pl.Buffered(1))
    tok_in = functools.partial(pl.BlockSpec, (tm, D), lambda i, e: (i, 0))
    tab = pl.BlockSpec((ec, D), lambda i, e: (e, 0))
    desc = pl.BlockSpec((heads, NK, tm), lambda i, e: (0, 0, i), **once)
    return pl.pallas_call(
        _peer_mix_kernel,
        grid=(T // tm, NE // ec),
        in_specs=[tok_in(**once), tab, tab, desc, desc, desc, desc, tok_in(**once)],
        out_specs=pl.BlockSpec((tm, D), lambda i, e: (i, 0)),
        out_shape=jax.ShapeDtypeStruct((T, D), F32),
        compiler_params=_cparams("parallel", "arbitrary"),
        name="peer_mix",
    )(hb, u_tab, v_tab, n0, e0, r1, e1, x)


def _rope_tables(pos, n_rows):
    half = DIFF_DC // 2
    inv = ROPE_THETA ** (-jnp.arange(half, dtype=F32) / half)
    ang = pos.astype(F32)[:, None] * inv[None, :]
    cos = jnp.cos(ang)
    sin = jnp.sin(ang)
    cos2 = jnp.concatenate([cos, cos], axis=-1)
    sin2 = jnp.concatenate([-sin, sin], axis=-1)
    if cos2.shape[0] != n_rows:
        cos2 = jnp.broadcast_to(cos2, (n_rows, DIFF_DC))
        sin2 = jnp.broadcast_to(sin2, (n_rows, DIFF_DC))
    return cos2, sin2


def _token_front(x, pos_tables, w, l_params):
    cos, sin = pos_tables
    h = rmsnorm_cast(x, l_params["norm1_g"])
    dscale = DIFF_DC ** -0.5
    fscale = FOX_DH ** -0.5
    wqkv, c = w["w_in"], w["cols"]
    (dq,) = project(h, wqkv, c["dq"], out_dtypes=(BF16,), gain=l_params["dq_g"], cos=cos, sin=sin, scale=dscale)
    dk32, dk16 = project(h, wqkv, c["dk"], out_dtypes=(F32, BF16), gain=l_params["dk_g"], cos=cos, sin=sin)
    dv32, dv16 = project(h, wqkv, c["dv"], out_dtypes=(F32, BF16))
    (fq,) = project(h, wqkv, c["fq"], out_dtypes=(BF16,), gain=l_params["fq_g"], scale=fscale)
    fk32, fk16 = project(h, wqkv, c["fk"], out_dtypes=(F32, BF16), gain=l_params["fk_g"])
    fv32, fv16 = project(h, wqkv, c["fv"], out_dtypes=(F32, BF16))
    ngate = w["gates"].shape[1] // 2
    (sga,) = project(h, w["gates"], (0, ngate), out_dtypes=(BF16,), act="sigmoid")
    (sgb,) = project(h, w["gates"], (ngate, ngate), out_dtypes=(BF16,), act="sigmoid")
    return h, dict(dq=dq, dk32=dk32, dk16=dk16, dv32=dv32, dv16=dv16, fq=fq, fk32=fk32, fk16=fk16,
                   fv32=fv32, fv16=fv16, sga=sga, sgb=sgb)


def _token_back(x, od, of, t, w, l_params):
    mix = merge(od, of, w["w_a"], w["w_b"], t["sga"], t["sgb"])
    x2 = out_proj_residual(mix, w["w_o"], x)
    hb, scores_t = peer_query(x2, l_params["norm2_g"], w["peer_wq"], w["peer_keys"])
    n0, e0, r1, e1 = peer_topk(scores_t)
    return peer_mix(hb, w["peer_u"], w["peer_v"], n0, e0, r1, e1, x2)


def kernel(x_prompt, x_sample, cache_diff_k, cache_diff_v, cache_fox_k, cache_fox_v, cache_fox_logf, page_table, norm1_g, w_in, b_f, dq_g, dk_g, fq_g, fk_g, lam_q1, lam_k1, lam_q2, lam_k2, dsub_g, w_a, w_b, w_o, norm2_g, peer_wq, peer_keys, peer_u, peer_v):
    B, S, D = x_prompt.shape
    DB, DS, _ = x_sample.shape
    assert DS == 1
    depth = w_in.shape[0]
    n_pages = page_table.shape[1]
    past = n_pages * PAGE_SIZE
    WD = DIFF_HEADS * DIFF_DV
    WF = FOX_HEADS * FOX_DH
    sizes = (WD, WD, WD, WF, WF, WF, FOX_HEADS, D, D)
    names = ("dq", "dk", "dv", "fq", "fk", "fv", "fl", "ga", "gb")
    offs = [0]
    for n in sizes:
        offs.append(offs[-1] + n)

    pos_p = _rope_tables(jnp.arange(S, dtype=jnp.int32), S)
    pos_s = _rope_tables(past + jnp.arange(DS, dtype=jnp.int32), DB)

    xp = x_prompt.reshape(B * S, D)
    xs = x_sample.reshape(DB * DS, D)
    outs = [[] for _ in range(10)]
    for l in range(depth):
        lam_init = 0.8 - 0.6 * math.exp(-0.3 * l)
        cols = {n: (offs[i], sizes[i]) for i, n in enumerate(names)}
        w = dict(w_in=w_in[l], cols=cols, fl=w_in[l, :, cols["fl"][0]:cols["ga"][0]], gates=w_in[l, :, cols["ga"][0]:],
                 w_a=w_a[l], w_b=w_b[l], w_o=w_o[l], peer_wq=peer_wq[l],
                 peer_keys=peer_keys[l].reshape(PEER_HEADS * 2, PEER_NKEYS, PEER_DHALF).astype(BF16),
                 peer_u=peer_u[l].astype(BF16), peer_v=peer_v[l].astype(BF16))
        lp = dict(norm1_g=norm1_g[l], dq_g=dq_g[l], dk_g=dk_g[l], fq_g=fq_g[l], fk_g=fk_g[l], norm2_g=norm2_g[l])
        lam_vecs = (lam_q1[l], lam_k1[l], lam_q2[l], lam_k2[l])

        h, t = _token_front(xp, pos_p, w, lp)
        logf, ct, ck = forget_gate(h, w["fl"], b_f[l], B, S)
        od = diff_attention(t["dq"], t["dk16"], t["dv16"], dsub_g[l], lam_vecs, lam_init, B, S)
        of = fox_attention(t["fq"], t["fk16"], t["fv16"], ct, ck, B, S)
        xp = _token_back(xp, od, of, t, w, lp)
        outs[0].append(t["dk32"].reshape(B, S, DIFF_HEADS, DIFF_DV))
        outs[1].append(t["dv32"].reshape(B, S, DIFF_HEADS, DIFF_DV))
        outs[2].append(t["fk32"].reshape(B, S, FOX_HEADS, FOX_DH))
        outs[3].append(t["fv32"].reshape(B, S, FOX_HEADS, FOX_DH))
        outs[4].append(logf)

        hs, ts_ = _token_front(xs, pos_s, w, lp)
        slogf, _, _ = forget_gate(hs, w["fl"], b_f[l], 1, DB)
        slogf = slogf.reshape(DB, DS, FOX_HEADS)
        tok_d = jnp.stack([ts_[n].astype(F32).reshape(DB, DIFF_HEADS, DIFF_DV) for n in ("dq", "dk16", "dv16")], axis=1)
        tok_f = jnp.stack([ts_[n].astype(F32).reshape(DB, FOX_HEADS, FOX_DH) for n in ("fq", "fk16", "fv16")], axis=1)
        sod, sof = decode_attention(
            page_table, l, tok_d, tok_f, slogf.reshape(DB, FOX_HEADS, 1),
            cache_diff_k, cache_diff_v, cache_fox_k, cache_fox_v, cache_fox_logf,
            dsub_g[l], lam_vecs, lam_init)
        sod = sod.astype(BF16)
        sof = sof.astype(BF16)
        xs = _token_back(xs, sod.reshape(DB, WD), sof.reshape(DB, WF), ts_, w, lp)
        outs[5].append(ts_["dk32"].reshape(DB, DS, DIFF_HEADS, DIFF_DV))
        outs[6].append(ts_["dv32"].reshape(DB, DS, DIFF_HEADS, DIFF_DV))
        outs[7].append(ts_["fk32"].reshape(DB, DS, FOX_HEADS, FOX_DH))
        outs[8].append(ts_["fv32"].reshape(DB, DS, FOX_HEADS, FOX_DH))
        outs[9].append(slogf)

    return (xp.reshape(B, S, D), xs.reshape(DB, DS, D)) + tuple(jnp.stack(o) for o in outs)
```

```python
import functools
import math

import jax
import jax.numpy as jnp
from jax import lax
from jax.experimental import pallas as pl
from jax.experimental.pallas import tpu as pltpu

DIFF_HEADS = 8
DIFF_DC = 128
DIFF_DV = 2 * DIFF_DC
FOX_HEADS = 16
FOX_DH = 128
PEER_HEADS = 8
PEER_NKEYS = 128
PEER_DHALF = 128
PEER_TOPK = 16
PAGE_SIZE = 128
ROPE_THETA = 10000.0
EPS = 1e-6
LANES = 128
VMEM_LIMIT = 56 * 1024 * 1024

F32 = jnp.float32
BF16 = jnp.bfloat16
NEG_INF = float("-inf")


def _cparams(*sem):
    return pltpu.CompilerParams(dimension_semantics=sem, vmem_limit_bytes=VMEM_LIMIT)


def _tile(n, pref):
    t = min(n, pref)
    while n % t:
        t //= 2
    return t


def _split3(x):
    hi = x.astype(BF16)
    r1 = x - hi.astype(F32)
    mid = r1.astype(BF16)
    lo = (r1 - mid.astype(F32)).astype(BF16)
    return hi, mid, lo


def _log_sigmoid(x):
    return jnp.minimum(x, 0.0) - jnp.log1p(jnp.exp(-jnp.abs(x)))


def _sigmoid(x):
    return 1.0 / (1.0 + jnp.exp(-x))


def _rmsnorm_kernel(x_ref, g_ref, o_ref):
    x = x_ref[...]
    ms = jnp.mean(x * x, axis=-1, keepdims=True)
    o_ref[...] = (x * lax.rsqrt(ms + EPS) * g_ref[...]).astype(o_ref.dtype)


def rmsnorm_cast(x, g):
    T, D = x.shape
    tm = _tile(T, 256)
    return pl.pallas_call(
        _rmsnorm_kernel,
        grid=(T // tm,),
        in_specs=[pl.BlockSpec((tm, D), lambda i: (i, 0)), pl.BlockSpec((1, D), lambda i: (0, 0))],
        out_specs=pl.BlockSpec((tm, D), lambda i: (i, 0)),
        out_shape=jax.ShapeDtypeStruct((T, D), BF16),
        compiler_params=_cparams("parallel"),
        name="rmsnorm_cast",
    )(x, g.reshape(1, D))


def _proj_kernel(*refs, norm, rope, act, scale):
    h_ref, w_ref = refs[0], refs[1]
    pos = 2
    gain_ref = cos_ref = sin_ref = None
    if norm:
        gain_ref = refs[pos]
        pos += 1
    if rope:
        cos_ref, sin_ref = refs[pos], refs[pos + 1]
        pos += 2
    out_refs = refs[pos:]
    z = jnp.dot(h_ref[...], w_ref[...], preferred_element_type=F32)
    tn = z.shape[1]
    if norm:
        for g in range(tn // LANES):
            x = z[:, g * LANES:(g + 1) * LANES]
            if rope:
                ones = jnp.ones((LANES, LANES), BF16)
                ms = sum(jnp.dot(p, ones, preferred_element_type=F32) for p in _split3(x * x)) * (1.0 / LANES)
            else:
                ms = jnp.mean(x * x, axis=-1, keepdims=True)
            y = x * lax.rsqrt(ms + EPS) * gain_ref[...]
            if rope:
                y = y * cos_ref[...] + pltpu.roll(y, LANES // 2, 1) * sin_ref[...]
            if scale != 1.0:
                y = y * scale
            for o_ref in out_refs:
                o_ref[:, g * LANES:(g + 1) * LANES] = y.astype(o_ref.dtype)
    else:
        if act == "sigmoid":
            z = _sigmoid(z)
        for o_ref in out_refs:
            o_ref[...] = z.astype(o_ref.dtype)


def project(h, w, *, out_dtypes, gain=None, cos=None, sin=None, act=None, scale=1.0, tm_pref=1024, tn_pref=512):
    T, D = h.shape
    N = w.shape[1]
    tm = _tile(T, tm_pref)
    tn = _tile(N, tn_pref)
    norm = gain is not None
    rope = cos is not None
    in_specs = [pl.BlockSpec((tm, D), lambda i, j: (i, 0)), pl.BlockSpec((D, tn), lambda i, j: (0, j))]
    args = [h, w]
    if norm:
        in_specs.append(pl.BlockSpec((1, LANES), lambda i, j: (0, 0)))
        args.append(gain.reshape(1, LANES))
    if rope:
        nrep = cos.shape[0] // tm
        in_specs += [pl.BlockSpec((tm, LANES), lambda i, j: (i % nrep, 0))] * 2
        args += [cos, sin]
    outs = pl.pallas_call(
        functools.partial(_proj_kernel, norm=norm, rope=rope, act=act, scale=scale),
        grid=(T // tm, N // tn),
        in_specs=in_specs,
        out_specs=[pl.BlockSpec((tm, tn), lambda i, j: (i, j)) for _ in out_dtypes],
        out_shape=[jax.ShapeDtypeStruct((T, N), dt) for dt in out_dtypes],
        compiler_params=_cparams("parallel", "arbitrary"),
        name="project",
    )(*args)
    return outs


def _forget_kernel(h_ref, w_ref, wt_ref, b_ref, bt_ref, logf_ref, ct_ref, ck_ref, carry, carry_t):
    s = pl.program_id(1)

    @pl.when(s == 0)
    def _():
        carry[...] = jnp.zeros_like(carry)
        carry_t[...] = jnp.zeros_like(carry_t)

    h = h_ref[...]
    ts = h.shape[0]
    fl = jnp.dot(h, w_ref[...], preferred_element_type=F32)
    logf = _log_sigmoid(fl + b_ref[...])
    logf_ref[...] = logf
    row = lax.broadcasted_iota(jnp.int32, (ts, ts), 0)
    col = lax.broadcasted_iota(jnp.int32, (ts, ts), 1)
    lower = (row >= col).astype(BF16)
    cum = sum(jnp.dot(lower, p, preferred_element_type=F32) for p in _split3(logf))
    c = cum + carry[...]
    carry[...] = c[ts - 1:ts, :]
    for hd in range(c.shape[1]):
        ck_ref[hd] = jnp.broadcast_to(c[:, hd:hd + 1], (ts, LANES))
    flt = lax.dot_general(wt_ref[...], h, (((1,), (1,)), ((), ())), preferred_element_type=F32)
    logft = _log_sigmoid(flt + bt_ref[...])
    upper = (row <= col).astype(BF16)
    cumt = sum(jnp.dot(p, upper, preferred_element_type=F32) for p in _split3(logft))
    ct = cumt + carry_t[...]
    ct_ref[...] = ct
    carry_t[...] = ct[:, ts - 1:ts]


def forget_gate(h, w_fl, b_f, B, S):
    D = h.shape[1]
    H = w_fl.shape[1]
    ts = _tile(S, 512)
    ns = S // ts
    return pl.pallas_call(
        _forget_kernel,
        grid=(B, ns),
        in_specs=[
            pl.BlockSpec((ts, D), lambda b, s: (b * ns + s, 0)),
            pl.BlockSpec((D, H), lambda b, s: (0, 0)),
            pl.BlockSpec((H, D), lambda b, s: (0, 0)),
            pl.BlockSpec((1, H), lambda b, s: (0, 0)),
            pl.BlockSpec((H, 1), lambda b, s: (0, 0)),
        ],
        out_specs=[
            pl.BlockSpec((None, ts, H), lambda b, s: (b, s, 0)),
            pl.BlockSpec((None, H, ts), lambda b, s: (b, 0, s)),
            pl.BlockSpec((None, H, ts, LANES), lambda b, s: (b, 0, s, 0)),
        ],
        out_shape=[
            jax.ShapeDtypeStruct((B, S, H), F32),
            jax.ShapeDtypeStruct((B, H, S), F32),
            jax.ShapeDtypeStruct((B, H, S, LANES), F32),
        ],
        scratch_shapes=[pltpu.VMEM((1, H), F32), pltpu.VMEM((H, 1), F32)],
        compiler_params=_cparams("parallel", "arbitrary"),
        name="forget_gate",
    )(h, w_fl, w_fl.T, b_f.reshape(1, H), b_f.reshape(H, 1))


def _lambda_value(lq1, lk1, lq2, lk2, lam_init):
    return (jnp.exp(jnp.sum(lq1 * lk1, axis=-1, keepdims=True))
            - jnp.exp(jnp.sum(lq2 * lk2, axis=-1, keepdims=True)) + lam_init)


_NT = (((1,), (1,)), ((), ()))
_TN = (((0,), (0,)), ((), ()))


def _softmax_step(s, v, m, l, acc):
    m_new = jnp.maximum(m, jnp.max(s, axis=0, keepdims=True))
    alpha = jnp.exp(m - m_new)
    p = jnp.exp(s - m_new)
    l = alpha * l + jnp.sum(p, axis=0, keepdims=True)
    acc = alpha * acc + lax.dot_general(v, p.astype(BF16), _TN, preferred_element_type=F32)
    return m_new, l, acc


def _causal_mask(s, key_offset):
    krow = lax.broadcasted_iota(jnp.int32, s.shape, 0)
    qcol = lax.broadcasted_iota(jnp.int32, s.shape, 1)
    return jnp.where(krow + key_offset <= qcol, s, NEG_INF)


def _causal_sweep(chunk, init, qi, tq, tk):
    n_full = qi * (tq // tk)
    carry = lax.fori_loop(0, n_full, lambda j, cr: chunk(j, cr, None), init)
    for d in range(tq // tk):
        carry = chunk(n_full + d, carry, d * tk)
    return carry


ATTN_TQ = 1024
ATTN_TK = 1024


def _diff_attn_kernel(q_ref, k_ref, v_ref, g_ref, lq1_ref, lk1_ref, lq2_ref, lk2_ref, o_ref, *, lam_init, tk):
    qi = pl.program_id(2)
    tq = q_ref.shape[0]
    q = q_ref[...]

    def chunk(j, carry, key_offset):
        rows = pl.ds(pl.multiple_of(j * tk, tk), tk)
        k = k_ref[rows, :]
        v = v_ref[rows, :]
        out = []
        for c in range(2):
            s = lax.dot_general(k[:, c * DIFF_DC:(c + 1) * DIFF_DC], q[:, c * DIFF_DC:(c + 1) * DIFF_DC], _NT,
                                preferred_element_type=F32)
            if key_offset is not None:
                s = _causal_mask(s, key_offset)
            out.append(_softmax_step(s, v, *carry[c]))
        return tuple(out)

    init = tuple((jnp.full((1, tq), NEG_INF, F32), jnp.zeros((1, tq), F32), jnp.zeros((DIFF_DV, tq), F32))
                 for _ in range(2))
    (_, l1, a1), (_, l2, a2) = _causal_sweep(chunk, init, qi, tq, tk)
    lam = _lambda_value(lq1_ref[...], lk1_ref[...], lq2_ref[...], lk2_ref[...], lam_init)
    o = a1 / l1 - lam * (a2 / l2)
    ms = jnp.mean(o * o, axis=0, keepdims=True)
    o_ref[...] = ((o * lax.rsqrt(ms + EPS)).T * g_ref[...] * (1.0 - lam_init)).astype(o_ref.dtype)


def diff_attention(q, k, v, dsub_g, lam_vecs, lam_init, B, S):
    tq = _tile(S, ATTN_TQ)
    tk = _tile(tq, ATTN_TK)
    n = S // tq
    W = DIFF_DV
    qspec = pl.BlockSpec((tq, W), lambda b, h, qi: (b * n + qi, h))
    kspec = pl.BlockSpec((S, W), lambda b, h, qi: (b, h))
    vec = pl.BlockSpec((1, DIFF_DC), lambda b, h, qi: (0, 0))
    return pl.pallas_call(
        functools.partial(_diff_attn_kernel, lam_init=lam_init, tk=tk),
        grid=(B, DIFF_HEADS, n),
        in_specs=[qspec, kspec, kspec, pl.BlockSpec((1, W), lambda b, h, qi: (0, 0)), vec, vec, vec, vec],
        out_specs=qspec,
        out_shape=jax.ShapeDtypeStruct((B * S, DIFF_HEADS * W), BF16),
        compiler_params=_cparams("parallel", "parallel", "arbitrary"),
        name="diff_attention",
    )(q, k, v, dsub_g.reshape(1, W), *[x.reshape(1, DIFF_DC) for x in lam_vecs])


def _fox_attn_kernel(q_ref, k_ref, v_ref, ct_ref, ck_ref, o_ref, *, tk):
    h = pl.program_id(1)
    qi = pl.program_id(2)
    tq = q_ref.shape[0]
    q = q_ref[...]
    cq = ct_ref[pl.ds(h, 1), :]

    def chunk(j, carry, key_offset):
        rows = pl.ds(pl.multiple_of(j * tk, tk), tk)
        s = lax.dot_general(k_ref[rows, :], q, _NT, preferred_element_type=F32)
        ck = ck_ref[rows, :]
        s = jnp.concatenate([s[:, i * LANES:(i + 1) * LANES] + (cq[:, i * LANES:(i + 1) * LANES] - ck)
                             for i in range(tq // LANES)], axis=1)
        if key_offset is not None:
            s = _causal_mask(s, key_offset)
        return _softmax_step(s, v_ref[rows, :], *carry)

    init = (jnp.full((1, tq), NEG_INF, F32), jnp.zeros((1, tq), F32), jnp.zeros((FOX_DH, tq), F32))
    _, l, acc = _causal_sweep(chunk, init, qi, tq, tk)
    o_ref[...] = (acc / l).T.astype(o_ref.dtype)


def fox_attention(q, k, v, ct, ck, B, S):
    tq = _tile(S, ATTN_TQ)
    tk = _tile(tq, ATTN_TK)
    n = S // tq
    H = FOX_HEADS
    qspec = pl.BlockSpec((tq, FOX_DH), lambda b, h, qi: (b * n + qi, h))
    kspec = pl.BlockSpec((S, FOX_DH), lambda b, h, qi: (b, h))
    return pl.pallas_call(
        functools.partial(_fox_attn_kernel, tk=tk),
        grid=(B, H, n),
        in_specs=[
            qspec, kspec, kspec,
            pl.BlockSpec((None, H, tq), lambda b, h, qi: (b, 0, qi)),
            pl.BlockSpec((None, None, S, LANES), lambda b, h, qi: (b, h, 0, 0)),
        ],
        out_specs=qspec,
        out_shape=jax.ShapeDtypeStruct((B * S, H * FOX_DH), BF16),
        compiler_params=_cparams("parallel", "parallel", "arbitrary"),
        name="fox_attention",
    )(q, k, v, ct, ck)


def _decode_kernel(*refs, lam_init, npp):
    pt_ref, td_ref, tf_ref, slogf_ref = refs[:4]
    pages = refs[4:4 + 5 * npp]
    ae_ref, g_ref, lamv_ref, od_ref, of_ref = refs[4 + 5 * npp:9 + 5 * npp]
    qd_rows, md, ld, accd, qf_rows, mf, lf, accf, carry = refs[9 + 5 * npp:]
    del pt_ref
    p = pl.program_id(1)
    HD = DIFF_HEADS
    HF = FOX_HEADS

    @pl.when(p == 0)
    def _():
        q8 = td_ref[0]
        lane = lax.broadcasted_iota(jnp.int32, q8.shape, 1)
        qr = jnp.concatenate([jnp.where(lane < DIFF_DC, q8, 0.0), jnp.where(lane >= DIFF_DC, q8, 0.0)], axis=0)
        qd_rows[...] = qr.astype(BF16)
        md[...] = jnp.sum(qr * jnp.concatenate([td_ref[1], td_ref[1]], axis=0), axis=-1, keepdims=True)
        ld[...] = jnp.ones_like(ld)
        accd[...] = jnp.concatenate([td_ref[2], td_ref[2]], axis=0)
        qf = tf_ref[0]
        qf_rows[...] = qf.astype(BF16)
        mf[...] = jnp.sum(qf * tf_ref[1], axis=-1, keepdims=True)
        lf[...] = jnp.ones_like(lf)
        accf[...] = tf_ref[2]
        carry[...] = jnp.zeros_like(carry)

    def update(scores, heads, values, m, l, acc):
        row = lax.broadcasted_iota(jnp.int32, scores[0].shape, 0)
        col = lax.broadcasted_iota(jnp.int32, scores[0].shape, 1)
        own = col % heads == row % heads
        scores = [jnp.where(own, s, NEG_INF) for s in scores]
        m_prev = m[...]
        m_new = m_prev
        for s in scores:
            m_new = jnp.maximum(m_new, jnp.max(s, axis=-1, keepdims=True))
        alpha = jnp.exp(m_prev - m_new)
        l_new = alpha * l[...]
        acc_new = alpha * acc[...]
        for s, v2 in zip(scores, values):
            pr = jnp.exp(s - m_new)
            l_new = l_new + jnp.sum(pr, axis=-1, keepdims=True)
            acc_new = acc_new + jnp.dot(pr.astype(BF16), v2, preferred_element_type=F32)
        l[...] = l_new
        acc[...] = acc_new
        m[...] = m_new

    nd = PAGE_SIZE * HD
    nf = PAGE_SIZE * HF
    sd, vd, sf, vf = [], [], [], []
    total = carry[...]
    for i in range(npp):
        pdk_ref, pdv_ref, pfk_ref, pfv_ref, plf_ref = pages[5 * i:5 * i + 5]
        kd2 = pdk_ref[...].reshape(nd, DIFF_DV).astype(BF16)
        sd.append(lax.dot_general(qd_rows[...], kd2, _NT, preferred_element_type=F32))
        vd.append(pdv_ref[...].reshape(nd, DIFF_DV).astype(BF16))
        b2 = sum(lax.dot_general(piece, ae_ref[...], _TN, preferred_element_type=F32)
                 for piece in _split3(plf_ref[...]))
        bias = slogf_ref[...] + total + b2[:, :nf]
        total = total + b2[:, nf:nf + 1]
        kf2 = pfk_ref[...].reshape(nf, FOX_DH).astype(BF16)
        sf.append(lax.dot_general(qf_rows[...], kf2, _NT, preferred_element_type=F32) + bias)
        vf.append(pfv_ref[...].reshape(nf, FOX_DH).astype(BF16))
    carry[...] = total
    update(sd, HD, vd, md, ld, accd)
    update(sf, HF, vf, mf, lf, accf)

    @pl.when(p == pl.num_programs(1) - 1)
    def _():
        lam = _lambda_value(lamv_ref[0:1, :], lamv_ref[1:2, :], lamv_ref[2:3, :], lamv_ref[3:4, :], lam_init)
        od = accd[...] / ld[...]
        o = od[:HD] - lam * od[HD:]
        ms = jnp.mean(o * o, axis=-1, keepdims=True)
        od_ref[...] = o * lax.rsqrt(ms + EPS) * g_ref[...] * (1.0 - lam_init)
        of_ref[...] = accf[...] / lf[...]


def _suffix_matrix(heads):
    r = jnp.arange(PAGE_SIZE)
    col_r = jnp.arange(PAGE_SIZE * heads) // heads
    after = r[:, None] > col_r[None, :]
    return jnp.concatenate([after, jnp.ones((PAGE_SIZE, LANES), bool)], axis=1).astype(BF16)


def decode_attention(page_table, layer, tok_d, tok_f, slogf_t, cache_dk, cache_dv, cache_fk, cache_fv,
                     cache_lf, dsub_g, lam_vecs, lam_init):
    DB, n_pages = page_table.shape
    HD, HF = DIFF_HEADS, FOX_HEADS
    npp = max(n for n in (4, 2, 1) if n_pages % n == 0)
    steps = n_pages // npp

    def page(i, *tail):
        zeros = (0,) * (len(tail) + 1)
        return pl.BlockSpec((None, None, PAGE_SIZE) + tail,
                            lambda b, p, pt: (layer, pt[b, n_pages - 1 - (p * npp + i)]) + zeros)

    pages, page_args = [], []
    for i in range(npp):
        pages += [page(i, HD, DIFF_DV), page(i, HD, DIFF_DV), page(i, HF, FOX_DH), page(i, HF, FOX_DH), page(i, HF)]
        page_args += [cache_dk, cache_dv, cache_fk, cache_fv, cache_lf]
    ae = _suffix_matrix(HF)
    out_d = pl.BlockSpec((None, HD, DIFF_DV), lambda b, p, pt: (b, 0, 0))
    out_f = pl.BlockSpec((None, HF, FOX_DH), lambda b, p, pt: (b, 0, 0))
    grid_spec = pltpu.PrefetchScalarGridSpec(
        num_scalar_prefetch=1,
        grid=(DB, steps),
        in_specs=[
            pl.BlockSpec((None, 3, HD, DIFF_DV), lambda b, p, pt: (b, 0, 0, 0)),
            pl.BlockSpec((None, 3, HF, FOX_DH), lambda b, p, pt: (b, 0, 0, 0)),
            pl.BlockSpec((None, HF, 1), lambda b, p, pt: (b, 0, 0)),
            *pages,
            pl.BlockSpec(ae.shape, lambda b, p, pt: (0, 0)),
            pl.BlockSpec((1, DIFF_DV), lambda b, p, pt: (0, 0)),
            pl.BlockSpec((4, DIFF_DC), lambda b, p, pt: (0, 0)),
        ],
        out_specs=[out_d, out_f],
        scratch_shapes=[
            pltpu.VMEM((2 * HD, DIFF_DV), BF16), pltpu.VMEM((2 * HD, 1), F32), pltpu.VMEM((2 * HD, 1), F32),
            pltpu.VMEM((2 * HD, DIFF_DV), F32),
            pltpu.VMEM((HF, FOX_DH), BF16), pltpu.VMEM((HF, 1), F32), pltpu.VMEM((HF, 1), F32),
            pltpu.VMEM((HF, FOX_DH), F32),
            pltpu.VMEM((HF, 1), F32),
        ],
    )
    return pl.pallas_call(
        functools.partial(_decode_kernel, lam_init=lam_init, npp=npp),
        grid_spec=grid_spec,
        out_shape=[jax.ShapeDtypeStruct((DB, HD, DIFF_DV), F32), jax.ShapeDtypeStruct((DB, HF, FOX_DH), F32)],
        compiler_params=_cparams("parallel", "arbitrary"),
        name="decode_attention",
    )(page_table, tok_d, tok_f, slogf_t, *page_args, ae, dsub_g.reshape(1, DIFF_DV), jnp.stack(lam_vecs))


def _merge_kernel(od_ref, of_ref, wa_ref, wb_ref, ga_ref, gb_ref, o_ref):
    ya = jnp.dot(od_ref[...], wa_ref[...], preferred_element_type=F32)
    yb = jnp.dot(of_ref[...], wb_ref[...], preferred_element_type=F32)
    o_ref[...] = (ga_ref[...].astype(F32) * ya + gb_ref[...].astype(F32) * yb).astype(o_ref.dtype)


def merge(od, of, w_a, w_b, sga, sgb):
    T, Wd = od.shape
    Wf = of.shape[1]
    D = w_a.shape[1]
    tm = _tile(T, 1024)
    tn = _tile(D, 512)
    return pl.pallas_call(
        _merge_kernel,
        grid=(T // tm, D // tn),
        in_specs=[
            pl.BlockSpec((tm, Wd), lambda i, j: (i, 0)), pl.BlockSpec((tm, Wf), lambda i, j: (i, 0)),
            pl.BlockSpec((Wd, tn), lambda i, j: (0, j)), pl.BlockSpec((Wf, tn), lambda i, j: (0, j)),
            pl.BlockSpec((tm, tn), lambda i, j: (i, j)), pl.BlockSpec((tm, tn), lambda i, j: (i, j)),
        ],
        out_specs=pl.BlockSpec((tm, tn), lambda i, j: (i, j)),
        out_shape=jax.ShapeDtypeStruct((T, D), BF16),
        compiler_params=_cparams("parallel", "arbitrary"),
        name="merge",
    )(od, of, w_a, w_b, sga, sgb)


def _out_proj_kernel(mix_ref, w_ref, x_ref, o_ref):
    o_ref[...] = x_ref[...] + jnp.dot(mix_ref[...], w_ref[...], preferred_element_type=F32)


def out_proj_residual(mix, w_o, x):
    T, D = mix.shape
    N = w_o.shape[1]
    tm = _tile(T, 1024)
    tn = _tile(N, 512)
    return pl.pallas_call(
        _out_proj_kernel,
        grid=(T // tm, N // tn),
        in_specs=[
            pl.BlockSpec((tm, D), lambda i, j: (i, 0)), pl.BlockSpec((D, tn), lambda i, j: (0, j)),
            pl.BlockSpec((tm, tn), lambda i, j: (i, j)),
        ],
        out_specs=pl.BlockSpec((tm, tn), lambda i, j: (i, j)),
        out_shape=jax.ShapeDtypeStruct((T, N), F32),
        compiler_params=_cparams("parallel", "arbitrary"),
        name="out_proj_residual",
    )(mix, w_o, x)


def _peer_query_kernel(x_ref, g_ref, wq_ref, keys_ref, hb_ref, st_ref, hb_scr):
    j = pl.program_id(1)

    @pl.when(j == 0)
    def _():
        x = x_ref[...]
        ms = jnp.mean(x * x, axis=-1, keepdims=True)
        hb = (x * lax.rsqrt(ms + EPS) * g_ref[...]).astype(BF16)
        hb_scr[...] = hb
        hb_ref[...] = hb

    q = jnp.dot(hb_scr[...], wq_ref[...], preferred_element_type=F32).astype(BF16)
    for g in range(q.shape[1] // PEER_DHALF):
        st_ref[g] = lax.dot_general(keys_ref[g], q[:, g * PEER_DHALF:(g + 1) * PEER_DHALF],
                                    (((1,), (1,)), ((), ())), preferred_element_type=F32)


def peer_query(x, g, wq, keys):
    T, D = x.shape
    NQ = wq.shape[1]
    tm = _tile(T, 512)
    gpt = 4
    tn = gpt * PEER_DHALF
    return pl.pallas_call(
        _peer_query_kernel,
        grid=(T // tm, NQ // tn),
        in_specs=[
            pl.BlockSpec((tm, D), lambda i, j: (i, 0)), pl.BlockSpec((1, D), lambda i, j: (0, 0)),
            pl.BlockSpec((D, tn), lambda i, j: (0, j)),
            pl.BlockSpec((gpt, PEER_NKEYS, PEER_DHALF), lambda i, j: (j, 0, 0)),
        ],
        out_specs=[pl.BlockSpec((tm, D), lambda i, j: (i, 0)),
                   pl.BlockSpec((gpt, PEER_NKEYS, tm), lambda i, j: (j, 0, i))],
        out_shape=[jax.ShapeDtypeStruct((T, D), BF16),
                   jax.ShapeDtypeStruct((NQ // PEER_DHALF, PEER_NKEYS, T), F32)],
        scratch_shapes=[pltpu.VMEM((tm, D), BF16)],
        compiler_params=_cparams("parallel", "arbitrary"),
        name="peer_query",
    )(x, g.reshape(1, D), wq, keys)


def _topk_rows(xs, n):
    R, tt = xs[0].shape
    rows = lax.broadcasted_iota(jnp.int32, (R, tt), 0).astype(F32)
    slot = lax.broadcasted_iota(jnp.int32, (n, tt), 0)

    def body(r, carry):
        out = []
        for x, rank, vals in carry:
            m = jnp.max(x, axis=0, keepdims=True)
            idx = jnp.min(jnp.where(x == m, rows, float(R)), axis=0, keepdims=True)
            hit = rows == idx
            out.append((jnp.where(hit, NEG_INF, x), jnp.where(hit, lax.convert_element_type(r, F32), rank),
                        jnp.where(slot == r, m, vals)))
        return tuple(out)

    init = tuple((x, jnp.full((R, tt), float(n), F32), jnp.zeros((n, tt), F32)) for x in xs)
    res = lax.fori_loop(0, n, body, init)
    return [(rank, vals) for _, rank, vals in res]


def _peer_topk_kernel(s_ref, n0_ref, e0_ref, r1_ref, e1_ref):
    K = PEER_TOPK
    KB = K // 2

    def head(h, carry):
        s0 = s_ref[2 * h]
        s1 = s_ref[2 * h + 1]
        (rank0, v0), (rank1, v1) = _topk_rows((s0, s1), K)
        cand = jnp.concatenate([v0[0:1, :] + v1] + [v0[a:a + 1, :] + v1[0:KB, :] for a in range(1, K)], axis=0)
        row = lax.broadcasted_iota(jnp.int32, cand.shape, 0)
        ra = jnp.where(row < K, 0, (row - K) // KB + 1)
        rb = jnp.where(row < K, row, (row - K) % KB)
        cand = jnp.where((ra + 1) * (rb + 1) <= K, cand, NEG_INF)
        ((crank, cvals),) = _topk_rows((cand,), K)
        sel = crank < float(K)
        z = jnp.sum(jnp.where(sel, jnp.exp(cand - cvals[0:1, :]), 0.0), axis=0, keepdims=True)
        n0 = jnp.zeros_like(s0)
        for a in range(K):
            lo, hi = (0, K) if a == 0 else (K + KB * (a - 1), K + KB * a)
            cnt = jnp.sum(jnp.where(sel[lo:hi, :], 1.0, 0.0), axis=0, keepdims=True)
            n0 = jnp.where(rank0 == float(a), cnt, n0)
        n0_ref[h] = n0
        e0_ref[h] = jnp.exp(s0 - v0[0:1, :])
        r1_ref[h] = rank1.astype(r1_ref.dtype)
        e1_ref[h] = (jnp.exp(s1 - v1[0:1, :]) / z).astype(e1_ref.dtype)
        return carry

    lax.fori_loop(0, s_ref.shape[0] // 2, head, 0)


def peer_topk(scores_t):
    G, NK, T = scores_t.shape
    tt = _tile(T, LANES)
    spec = pl.BlockSpec((G // 2, NK, tt), lambda i: (0, 0, i))
    return pl.pallas_call(
        _peer_topk_kernel,
        grid=(T // tt,),
        in_specs=[pl.BlockSpec((G, NK, tt), lambda i: (0, 0, i))],
        out_specs=[spec] * 4,
        out_shape=[jax.ShapeDtypeStruct((G // 2, NK, T), dt) for dt in (F32, F32, BF16, BF16)],
        compiler_params=_cparams("parallel"),
        name="peer_topk",
    )(scores_t)


def _peer_mix_kernel(hb_ref, u_ref, v_ref, n0_ref, e0_ref, r1_ref, e1_ref, x_ref, o_ref):
    e = pl.program_id(1)

    @pl.when(e == 0)
    def _():
        o_ref[...] = x_ref[...]

    NK = PEER_NKEYS
    ipc = u_ref.shape[0] // NK
    at = lax.dot_general(u_ref[...], hb_ref[...], (((1,), (1,)), ((), ())), preferred_element_type=F32)
    act = (0.5 * at * (1.0 + lax.erf(at * (2.0 ** -0.5)))).astype(BF16)
    zero = jnp.zeros((), r1_ref.dtype)
    parts = []
    for ii in range(ipc):
        i = e * ipc + ii
        gate = None
        for h in range(n0_ref.shape[0]):
            n0row = n0_ref[h, pl.ds(i, 1), :].astype(r1_ref.dtype)
            e0row = e0_ref[h, pl.ds(i, 1), :].astype(e1_ref.dtype)
            term = jnp.where(r1_ref[h] < n0row, e1_ref[h], zero) * e0row
            gate = term if gate is None else gate + term
        parts.append(gate * act[ii * NK:(ii + 1) * NK, :])
    wt = jnp.concatenate(parts, axis=0)
    o_ref[...] += lax.dot_general(wt, v_ref[...], (((0,), (0,)), ((), ())), preferred_element_type=F32)


def peer_mix(hb, u_tab, v_tab, n0, e0, r1, e1, x):
    T, D = hb.shape
    NE = u_tab.shape[0]
    heads, NK, _ = n0.shape
    tm = _tile(T, 512)
    ec = 4 * NK
    once = dict(pipeline_mode=pl.Buffered(1))
    tok_in = functools.partial(pl.BlockSpec, (tm, D), lambda i, e: (i, 0))
    tab = pl.BlockSpec((ec, D), lambda i, e: (e, 0))
    desc = pl.BlockSpec((heads, NK, tm), lambda i, e: (0, 0, i), **once)
    return pl.pallas_call(
        _peer_mix_kernel,
        grid=(T // tm, NE // ec),
        in_specs=[tok_in(**once), tab, tab, desc, desc, desc, desc, tok_in(**once)],
        out_specs=pl.BlockSpec((tm, D), lambda i, e: (i, 0)),
        out_shape=jax.ShapeDtypeStruct((T, D), F32),
        compiler_params=_cparams("parallel", "arbitrary"),
        name="peer_mix",
    )(hb, u_tab, v_tab, n0, e0, r1, e1, x)


def _rope_tables(pos, n_rows):
    half = DIFF_DC // 2
    inv = ROPE_THETA ** (-jnp.arange(half, dtype=F32) / half)
    ang = pos.astype(F32)[:, None] * inv[None, :]
    cos = jnp.cos(ang)
    sin = jnp.sin(ang)
    cos2 = jnp.concatenate([cos, cos], axis=-1)
    sin2 = jnp.concatenate([-sin, sin], axis=-1)
    if cos2.shape[0] != n_rows:
        cos2 = jnp.broadcast_to(cos2, (n_rows, DIFF_DC))
        sin2 = jnp.broadcast_to(sin2, (n_rows, DIFF_DC))
    return cos2, sin2


def _token_front(x, pos_tables, w, l_params):
    cos, sin = pos_tables
    h = rmsnorm_cast(x, l_params["norm1_g"])
    dscale = DIFF_DC ** -0.5
    fscale = FOX_DH ** -0.5
    (dq,) = project(h, w["dq"], out_dtypes=(BF16,), gain=l_params["dq_g"], cos=cos, sin=sin, scale=dscale)
    dk32, dk16 = project(h, w["dk"], out_dtypes=(F32, BF16), gain=l_params["dk_g"], cos=cos, sin=sin)
    dv32, dv16 = project(h, w["dv"], out_dtypes=(F32, BF16))
    (fq,) = project(h, w["fq"], out_dtypes=(BF16,), gain=l_params["fq_g"], scale=fscale)
    fk32, fk16 = project(h, w["fk"], out_dtypes=(F32, BF16), gain=l_params["fk_g"])
    fv32, fv16 = project(h, w["fv"], out_dtypes=(F32, BF16))
    (sga,) = project(h, w["ga"], out_dtypes=(BF16,), act="sigmoid")
    (sgb,) = project(h, w["gb"], out_dtypes=(BF16,), act="sigmoid")
    return h, dict(dq=dq, dk32=dk32, dk16=dk16, dv32=dv32, dv16=dv16, fq=fq, fk32=fk32, fk16=fk16,
                   fv32=fv32, fv16=fv16, sga=sga, sgb=sgb)


def _token_back(x, od, of, t, w, l_params):
    mix = merge(od, of, w["w_a"], w["w_b"], t["sga"], t["sgb"])
    x2 = out_proj_residual(mix, w["w_o"], x)
    hb, scores_t = peer_query(x2, l_params["norm2_g"], w["peer_wq"], w["peer_keys"])
    n0, e0, r1, e1 = peer_topk(scores_t)
    return peer_mix(hb, w["peer_u"], w["peer_v"], n0, e0, r1, e1, x2)


def kernel(x_prompt, x_sample, cache_diff_k, cache_diff_v, cache_fox_k, cache_fox_v, cache_fox_logf, page_table, norm1_g, w_in, b_f, dq_g, dk_g, fq_g, fk_g, lam_q1, lam_k1, lam_q2, lam_k2, dsub_g, w_a, w_b, w_o, norm2_g, peer_wq, peer_keys, peer_u, peer_v):
    B, S, D = x_prompt.shape
    DB, DS, _ = x_sample.shape
    assert DS == 1
    depth = w_in.shape[0]
    n_pages = page_table.shape[1]
    past = n_pages * PAGE_SIZE
    WD = DIFF_HEADS * DIFF_DV
    WF = FOX_HEADS * FOX_DH
    sizes = (WD, WD, WD, WF, WF, WF, FOX_HEADS, D, D)
    names = ("dq", "dk", "dv", "fq", "fk", "fv", "fl", "ga", "gb")
    offs = [0]
    for n in sizes:
        offs.append(offs[-1] + n)

    pos_p = _rope_tables(jnp.arange(S, dtype=jnp.int32), S)
    pos_s = _rope_tables(past + jnp.arange(DS, dtype=jnp.int32), DB)

    xp = x_prompt.reshape(B * S, D)
    xs = x_sample.reshape(DB * DS, D)
    outs = [[] for _ in range(10)]
    for l in range(depth):
        lam_init = 0.8 - 0.6 * math.exp(-0.3 * l)
        w = {n: w_in[l, :, offs[i]:offs[i + 1]].astype(BF16) for i, n in enumerate(names)}
        w.update(w_a=w_a[l].astype(BF16), w_b=w_b[l].astype(BF16), w_o=w_o[l].astype(BF16),
                 peer_wq=peer_wq[l].astype(BF16),
                 peer_keys=peer_keys[l].reshape(PEER_HEADS * 2, PEER_NKEYS, PEER_DHALF).astype(BF16),
                 peer_u=peer_u[l].astype(BF16), peer_v=peer_v[l].astype(BF16))
        lp = dict(norm1_g=norm1_g[l], dq_g=dq_g[l], dk_g=dk_g[l], fq_g=fq_g[l], fk_g=fk_g[l], norm2_g=norm2_g[l])
        lam_vecs = (lam_q1[l], lam_k1[l], lam_q2[l], lam_k2[l])

        h, t = _token_front(xp, pos_p, w, lp)
        logf, ct, ck = forget_gate(h, w["fl"], b_f[l], B, S)
        od = diff_attention(t["dq"], t["dk16"], t["dv16"], dsub_g[l], lam_vecs, lam_init, B, S)
        of = fox_attention(t["fq"], t["fk16"], t["fv16"], ct, ck, B, S)
        xp = _token_back(xp, od, of, t, w, lp)
        outs[0].append(t["dk32"].reshape(B, S, DIFF_HEADS, DIFF_DV))
        outs[1].append(t["dv32"].reshape(B, S, DIFF_HEADS, DIFF_DV))
        outs[2].append(t["fk32"].reshape(B, S, FOX_HEADS, FOX_DH))
        outs[3].append(t["fv32"].reshape(B, S, FOX_HEADS, FOX_DH))
        outs[4].append(logf)

        hs, ts_ = _token_front(xs, pos_s, w, lp)
        slogf, _, _ = forget_gate(hs, w["fl"], b_f[l], 1, DB)
        slogf = slogf.reshape(DB, DS, FOX_HEADS)
        tok_d = jnp.stack([ts_[n].astype(F32).reshape(DB, DIFF_HEADS, DIFF_DV) for n in ("dq", "dk16", "dv16")], axis=1)
        tok_f = jnp.stack([ts_[n].astype(F32).reshape(DB, FOX_HEADS, FOX_DH) for n in ("fq", "fk16", "fv16")], axis=1)
        sod, sof = decode_attention(
            page_table, l, tok_d, tok_f, slogf.reshape(DB, FOX_HEADS, 1),
            cache_diff_k, cache_diff_v, cache_fox_k, cache_fox_v, cache_fox_logf,
            dsub_g[l], lam_vecs, lam_init)
        sod = sod.astype(BF16)
        sof = sof.astype(BF16)
        xs = _token_back(xs, sod.reshape(DB, WD), sof.reshape(DB, WF), ts_, w, lp)
        outs[5].append(ts_["dk32"].reshape(DB, DS, DIFF_HEADS, DIFF_DV))
        outs[6].append(ts_["dv32"].reshape(DB, DS, DIFF_HEADS, DIFF_DV))
        outs[7].append(ts_["fk32"].reshape(DB, DS, FOX_HEADS, FOX_DH))
        outs[8].append(ts_["fv32"].reshape(DB, DS, FOX_HEADS, FOX_DH))
        outs[9].append(slogf)

    return (xp.reshape(B, S, D), xs.reshape(DB, DS, D)) + tuple(jnp.stack(o) for o in outs)
```

```python
import functools
import math

import jax
import jax.numpy as jnp
from jax import lax
from jax.experimental import pallas as pl
from jax.experimental.pallas import tpu as pltpu

DIFF_HEADS = 8
DIFF_DC = 128
DIFF_DV = 2 * DIFF_DC
FOX_HEADS = 16
FOX_DH = 128
PEER_HEADS = 8
PEER_NKEYS = 128
PEER_DHALF = 128
PEER_TOPK = 16
PAGE_SIZE = 128
ROPE_THETA = 10000.0
EPS = 1e-6
LANES = 128
VMEM_LIMIT = 56 * 1024 * 1024

F32 = jnp.float32
BF16 = jnp.bfloat16
NEG_INF = float("-inf")


def _cparams(*sem):
    return pltpu.CompilerParams(dimension_semantics=sem, vmem_limit_bytes=VMEM_LIMIT)


def _tile(n, pref):
    t = min(n, pref)
    while n % t:
        t //= 2
    return t


def _split3(x):
    hi = x.astype(BF16)
    r1 = x - hi.astype(F32)
    mid = r1.astype(BF16)
    lo = (r1 - mid.astype(F32)).astype(BF16)
    return hi, mid, lo


def _log_sigmoid(x):
    return jnp.minimum(x, 0.0) - jnp.log1p(jnp.exp(-jnp.abs(x)))


def _sigmoid(x):
    return 1.0 / (1.0 + jnp.exp(-x))


def _rmsnorm_kernel(x_ref, g_ref, o_ref):
    x = x_ref[...]
    ms = jnp.mean(x * x, axis=-1, keepdims=True)
    o_ref[...] = (x * lax.rsqrt(ms + EPS) * g_ref[...]).astype(o_ref.dtype)


def rmsnorm_cast(x, g):
    T, D = x.shape
    tm = _tile(T, 256)
    return pl.pallas_call(
        _rmsnorm_kernel,
        grid=(T // tm,),
        in_specs=[pl.BlockSpec((tm, D), lambda i: (i, 0)), pl.BlockSpec((1, D), lambda i: (0, 0))],
        out_specs=pl.BlockSpec((tm, D), lambda i: (i, 0)),
        out_shape=jax.ShapeDtypeStruct((T, D), BF16),
        compiler_params=_cparams("parallel"),
        name="rmsnorm_cast",
    )(x, g.reshape(1, D))


def _proj_kernel(*refs, norm, rope, act, scale):
    h_ref, w_ref = refs[0], refs[1]
    pos = 2
    gain_ref = cos_ref = sin_ref = None
    if norm:
        gain_ref = refs[pos]
        pos += 1
    if rope:
        cos_ref, sin_ref = refs[pos], refs[pos + 1]
        pos += 2
    out_refs = refs[pos:]
    z = jnp.dot(h_ref[...], w_ref[...], preferred_element_type=F32)
    tn = z.shape[1]
    if norm:
        for g in range(tn // LANES):
            x = z[:, g * LANES:(g + 1) * LANES]
            if rope:
                ones = jnp.ones((LANES, LANES), BF16)
                ms = sum(jnp.dot(p, ones, preferred_element_type=F32) for p in _split3(x * x)) * (1.0 / LANES)
            else:
                ms = jnp.mean(x * x, axis=-1, keepdims=True)
            y = x * lax.rsqrt(ms + EPS) * gain_ref[...]
            if rope:
                y = y * cos_ref[...] + pltpu.roll(y, LANES // 2, 1) * sin_ref[...]
            if scale != 1.0:
                y = y * scale
            for o_ref in out_refs:
                o_ref[:, g * LANES:(g + 1) * LANES] = y.astype(o_ref.dtype)
    else:
        if act == "sigmoid":
            z = _sigmoid(z)
        for o_ref in out_refs:
            o_ref[...] = z.astype(o_ref.dtype)


def project(h, w, *, out_dtypes, gain=None, cos=None, sin=None, act=None, scale=1.0, tm_pref=1024, tn_pref=512):
    T, D = h.shape
    N = w.shape[1]
    tm = _tile(T, tm_pref)
    tn = _tile(N, tn_pref)
    norm = gain is not None
    rope = cos is not None
    in_specs = [pl.BlockSpec((tm, D), lambda i, j: (i, 0)), pl.BlockSpec((D, tn), lambda i, j: (0, j))]
    args = [h, w]
    if norm:
        in_specs.append(pl.BlockSpec((1, LANES), lambda i, j: (0, 0)))
        args.append(gain.reshape(1, LANES))
    if rope:
        nrep = cos.shape[0] // tm
        in_specs += [pl.BlockSpec((tm, LANES), lambda i, j: (i % nrep, 0))] * 2
        args += [cos, sin]
    outs = pl.pallas_call(
        functools.partial(_proj_kernel, norm=norm, rope=rope, act=act, scale=scale),
        grid=(T // tm, N // tn),
        in_specs=in_specs,
        out_specs=[pl.BlockSpec((tm, tn), lambda i, j: (i, j)) for _ in out_dtypes],
        out_shape=[jax.ShapeDtypeStruct((T, N), dt) for dt in out_dtypes],
        compiler_params=_cparams("parallel", "arbitrary"),
        name="project",
    )(*args)
    return outs


def _forget_kernel(h_ref, w_ref, wt_ref, b_ref, bt_ref, logf_ref, ct_ref, ck_ref, carry, carry_t):
    s = pl.program_id(1)

    @pl.when(s == 0)
    def _():
        carry[...] = jnp.zeros_like(carry)
        carry_t[...] = jnp.zeros_like(carry_t)

    h = h_ref[...]
    ts = h.shape[0]
    fl = jnp.dot(h, w_ref[...], preferred_element_type=F32)
    logf = _log_sigmoid(fl + b_ref[...])
    logf_ref[...] = logf
    row = lax.broadcasted_iota(jnp.int32, (ts, ts), 0)
    col = lax.broadcasted_iota(jnp.int32, (ts, ts), 1)
    lower = (row >= col).astype(BF16)
    cum = sum(jnp.dot(lower, p, preferred_element_type=F32) for p in _split3(logf))
    c = cum + carry[...]
    carry[...] = c[ts - 1:ts, :]
    for hd in range(c.shape[1]):
        ck_ref[hd] = jnp.broadcast_to(c[:, hd:hd + 1], (ts, LANES))
    flt = lax.dot_general(wt_ref[...], h, (((1,), (1,)), ((), ())), preferred_element_type=F32)
    logft = _log_sigmoid(flt + bt_ref[...])
    upper = (row <= col).astype(BF16)
    cumt = sum(jnp.dot(p, upper, preferred_element_type=F32) for p in _split3(logft))
    ct = cumt + carry_t[...]
    ct_ref[...] = ct
    carry_t[...] = ct[:, ts - 1:ts]


def forget_gate(h, w_fl, b_f, B, S):
    D = h.shape[1]
    H = w_fl.shape[1]
    ts = _tile(S, 512)
    ns = S // ts
    return pl.pallas_call(
        _forget_kernel,
        grid=(B, ns),
        in_specs=[
            pl.BlockSpec((ts, D), lambda b, s: (b * ns + s, 0)),
            pl.BlockSpec((D, H), lambda b, s: (0, 0)),
            pl.BlockSpec((H, D), lambda b, s: (0, 0)),
            pl.BlockSpec((1, H), lambda b, s: (0, 0)),
            pl.BlockSpec((H, 1), lambda b, s: (0, 0)),
        ],
        out_specs=[
            pl.BlockSpec((None, ts, H), lambda b, s: (b, s, 0)),
            pl.BlockSpec((None, H, ts), lambda b, s: (b, 0, s)),
            pl.BlockSpec((None, H, ts, LANES), lambda b, s: (b, 0, s, 0)),
        ],
        out_shape=[
            jax.ShapeDtypeStruct((B, S, H), F32),
            jax.ShapeDtypeStruct((B, H, S), F32),
            jax.ShapeDtypeStruct((B, H, S, LANES), F32),
        ],
        scratch_shapes=[pltpu.VMEM((1, H), F32), pltpu.VMEM((H, 1), F32)],
        compiler_params=_cparams("parallel", "arbitrary"),
        name="forget_gate",
    )(h, w_fl, w_fl.T, b_f.reshape(1, H), b_f.reshape(H, 1))


def _lambda_value(lq1, lk1, lq2, lk2, lam_init):
    return (jnp.exp(jnp.sum(lq1 * lk1, axis=-1, keepdims=True))
            - jnp.exp(jnp.sum(lq2 * lk2, axis=-1, keepdims=True)) + lam_init)


_NT = (((1,), (1,)), ((), ()))
_TN = (((0,), (0,)), ((), ()))


def _softmax_step(s, v, m, l, acc):
    m_new = jnp.maximum(m, jnp.max(s, axis=0, keepdims=True))
    alpha = jnp.exp(m - m_new)
    p = jnp.exp(s - m_new)
    l = alpha * l + jnp.sum(p, axis=0, keepdims=True)
    acc = alpha * acc + lax.dot_general(v, p.astype(BF16), _TN, preferred_element_type=F32)
    return m_new, l, acc


def _causal_mask(s, key_offset):
    krow = lax.broadcasted_iota(jnp.int32, s.shape, 0)
    qcol = lax.broadcasted_iota(jnp.int32, s.shape, 1)
    return jnp.where(krow + key_offset <= qcol, s, NEG_INF)


def _causal_sweep(chunk, init, qi, tq, tk):
    n_full = qi * (tq // tk)
    carry = lax.fori_loop(0, n_full, lambda j, cr: chunk(j, cr, None), init)
    for d in range(tq // tk):
        carry = chunk(n_full + d, carry, d * tk)
    return carry


ATTN_TQ = 1024
ATTN_TK = 1024


def _diff_attn_kernel(q_ref, k_ref, v_ref, g_ref, lq1_ref, lk1_ref, lq2_ref, lk2_ref, o_ref, *, lam_init, tk):
    qi = pl.program_id(2)
    tq = q_ref.shape[0]
    q = q_ref[...]

    def chunk(j, carry, key_offset):
        rows = pl.ds(pl.multiple_of(j * tk, tk), tk)
        k = k_ref[rows, :]
        v = v_ref[rows, :]
        out = []
        for c in range(2):
            s = lax.dot_general(k[:, c * DIFF_DC:(c + 1) * DIFF_DC], q[:, c * DIFF_DC:(c + 1) * DIFF_DC], _NT,
                                preferred_element_type=F32)
            if key_offset is not None:
                s = _causal_mask(s, key_offset)
            out.append(_softmax_step(s, v, *carry[c]))
        return tuple(out)

    init = tuple((jnp.full((1, tq), NEG_INF, F32), jnp.zeros((1, tq), F32), jnp.zeros((DIFF_DV, tq), F32))
                 for _ in range(2))
    (_, l1, a1), (_, l2, a2) = _causal_sweep(chunk, init, qi, tq, tk)
    lam = _lambda_value(lq1_ref[...], lk1_ref[...], lq2_ref[...], lk2_ref[...], lam_init)
    o = a1 / l1 - lam * (a2 / l2)
    ms = jnp.mean(o * o, axis=0, keepdims=True)
    o_ref[...] = ((o * lax.rsqrt(ms + EPS)).T * g_ref[...] * (1.0 - lam_init)).astype(o_ref.dtype)


def diff_attention(q, k, v, dsub_g, lam_vecs, lam_init, B, S):
    tq = _tile(S, ATTN_TQ)
    tk = _tile(tq, ATTN_TK)
    n = S // tq
    W = DIFF_DV
    qspec = pl.BlockSpec((tq, W), lambda b, h, qi: (b * n + qi, h))
    kspec = pl.BlockSpec((S, W), lambda b, h, qi: (b, h))
    vec = pl.BlockSpec((1, DIFF_DC), lambda b, h, qi: (0, 0))
    return pl.pallas_call(
        functools.partial(_diff_attn_kernel, lam_init=lam_init, tk=tk),
        grid=(B, DIFF_HEADS, n),
        in_specs=[qspec, kspec, kspec, pl.BlockSpec((1, W), lambda b, h, qi: (0, 0)), vec, vec, vec, vec],
        out_specs=qspec,
        out_shape=jax.ShapeDtypeStruct((B * S, DIFF_HEADS * W), BF16),
        compiler_params=_cparams("parallel", "parallel", "arbitrary"),
        name="diff_attention",
    )(q, k, v, dsub_g.reshape(1, W), *[x.reshape(1, DIFF_DC) for x in lam_vecs])


def _fox_attn_kernel(q_ref, k_ref, v_ref, ct_ref, ck_ref, o_ref, *, tk):
    h = pl.program_id(1)
    qi = pl.program_id(2)
    tq = q_ref.shape[0]
    q = q_ref[...]
    cq = ct_ref[pl.ds(h, 1), :]

    def chunk(j, carry, key_offset):
        rows = pl.ds(pl.multiple_of(j * tk, tk), tk)
        s = lax.dot_general(k_ref[rows, :], q, _NT, preferred_element_type=F32)
        ck = ck_ref[rows, :]
        s = jnp.concatenate([s[:, i * LANES:(i + 1) * LANES] + (cq[:, i * LANES:(i + 1) * LANES] - ck)
                             for i in range(tq // LANES)], axis=1)
        if key_offset is not None:
            s = _causal_mask(s, key_offset)
        return _softmax_step(s, v_ref[rows, :], *carry)

    init = (jnp.full((1, tq), NEG_INF, F32), jnp.zeros((1, tq), F32), jnp.zeros((FOX_DH, tq), F32))
    _, l, acc = _causal_sweep(chunk, init, qi, tq, tk)
    o_ref[...] = (acc / l).T.astype(o_ref.dtype)


def fox_attention(q, k, v, ct, ck, B, S):
    tq = _tile(S, ATTN_TQ)
    tk = _tile(tq, ATTN_TK)
    n = S // tq
    H = FOX_HEADS
    qspec = pl.BlockSpec((tq, FOX_DH), lambda b, h, qi: (b * n + qi, h))
    kspec = pl.BlockSpec((S, FOX_DH), lambda b, h, qi: (b, h))
    return pl.pallas_call(
        functools.partial(_fox_attn_kernel, tk=tk),
        grid=(B, H, n),
        in_specs=[
            qspec, kspec, kspec,
            pl.BlockSpec((None, H, tq), lambda b, h, qi: (b, 0, qi)),
            pl.BlockSpec((None, None, S, LANES), lambda b, h, qi: (b, h, 0, 0)),
        ],
        out_specs=qspec,
        out_shape=jax.ShapeDtypeStruct((B * S, H * FOX_DH), BF16),
        compiler_params=_cparams("parallel", "parallel", "arbitrary"),
        name="fox_attention",
    )(q, k, v, ct, ck)


def _decode_kernel(*refs, lam_init, npp):
    pt_ref, td_ref, tf_ref, slogf_ref = refs[:4]
    pages = refs[4:4 + 5 * npp]
    ae_ref, g_ref, lamv_ref, od_ref, of_ref = refs[4 + 5 * npp:9 + 5 * npp]
    qd_rows, md, ld, accd, qf_rows, mf, lf, accf, carry = refs[9 + 5 * npp:]
    del pt_ref
    p = pl.program_id(1)
    HD = DIFF_HEADS
    HF = FOX_HEADS

    @pl.when(p == 0)
    def _():
        q8 = td_ref[0]
        lane = lax.broadcasted_iota(jnp.int32, q8.shape, 1)
        qr = jnp.concatenate([jnp.where(lane < DIFF_DC, q8, 0.0), jnp.where(lane >= DIFF_DC, q8, 0.0)], axis=0)
        qd_rows[...] = qr.astype(BF16)
        md[...] = jnp.sum(qr * jnp.concatenate([td_ref[1], td_ref[1]], axis=0), axis=-1, keepdims=True)
        ld[...] = jnp.ones_like(ld)
        accd[...] = jnp.concatenate([td_ref[2], td_ref[2]], axis=0)
        qf = tf_ref[0]
        qf_rows[...] = qf.astype(BF16)
        mf[...] = jnp.sum(qf * tf_ref[1], axis=-1, keepdims=True)
        lf[...] = jnp.ones_like(lf)
        accf[...] = tf_ref[2]
        carry[...] = jnp.zeros_like(carry)

    def update(scores, heads, values, m, l, acc):
        row = lax.broadcasted_iota(jnp.int32, scores[0].shape, 0)
        col = lax.broadcasted_iota(jnp.int32, scores[0].shape, 1)
        own = col % heads == row % heads
        scores = [jnp.where(own, s, NEG_INF) for s in scores]
        m_prev = m[...]
        m_new = m_prev
        for s in scores:
            m_new = jnp.maximum(m_new, jnp.max(s, axis=-1, keepdims=True))
        alpha = jnp.exp(m_prev - m_new)
        l_new = alpha * l[...]
        acc_new = alpha * acc[...]
        for s, v2 in zip(scores, values):
            pr = jnp.exp(s - m_new)
            l_new = l_new + jnp.sum(pr, axis=-1, keepdims=True)
            acc_new = acc_new + jnp.dot(pr.astype(BF16), v2, preferred_element_type=F32)
        l[...] = l_new
        acc[...] = acc_new
        m[...] = m_new

    nd = PAGE_SIZE * HD
    nf = PAGE_SIZE * HF
    sd, vd, sf, vf = [], [], [], []
    total = carry[...]
    for i in range(npp):
        pdk_ref, pdv_ref, pfk_ref, pfv_ref, plf_ref = pages[5 * i:5 * i + 5]
        kd2 = pdk_ref[...].reshape(nd, DIFF_DV).astype(BF16)
        sd.append(lax.dot_general(qd_rows[...], kd2, _NT, preferred_element_type=F32))
        vd.append(pdv_ref[...].reshape(nd, DIFF_DV).astype(BF16))
        b2 = sum(lax.dot_general(piece, ae_ref[...], _TN, preferred_element_type=F32)
                 for piece in _split3(plf_ref[...]))
        bias = slogf_ref[...] + total + b2[:, :nf]
        total = total + b2[:, nf:nf + 1]
        kf2 = pfk_ref[...].reshape(nf, FOX_DH).astype(BF16)
        sf.append(lax.dot_general(qf_rows[...], kf2, _NT, preferred_element_type=F32) + bias)
        vf.append(pfv_ref[...].reshape(nf, FOX_DH).astype(BF16))
    carry[...] = total
    update(sd, HD, vd, md, ld, accd)
    update(sf, HF, vf, mf, lf, accf)

    @pl.when(p == pl.num_programs(1) - 1)
    def _():
        lam = _lambda_value(lamv_ref[0:1, :], lamv_ref[1:2, :], lamv_ref[2:3, :], lamv_ref[3:4, :], lam_init)
        od = accd[...] / ld[...]
        o = od[:HD] - lam * od[HD:]
        ms = jnp.mean(o * o, axis=-1, keepdims=True)
        od_ref[...] = o * lax.rsqrt(ms + EPS) * g_ref[...] * (1.0 - lam_init)
        of_ref[...] = accf[...] / lf[...]


def _suffix_matrix(heads):
    r = jnp.arange(PAGE_SIZE)
    col_r = jnp.arange(PAGE_SIZE * heads) // heads
    after = r[:, None] > col_r[None, :]
    return jnp.concatenate([after, jnp.ones((PAGE_SIZE, LANES), bool)], axis=1).astype(BF16)


def decode_attention(page_table, layer, tok_d, tok_f, slogf_t, cache_dk, cache_dv, cache_fk, cache_fv,
                     cache_lf, dsub_g, lam_vecs, lam_init):
    DB, n_pages = page_table.shape
    HD, HF = DIFF_HEADS, FOX_HEADS
    npp = max(n for n in (4, 2, 1) if n_pages % n == 0)
    steps = n_pages // npp

    def page(i, *tail):
        zeros = (0,) * (len(tail) + 1)
        return pl.BlockSpec((None, None, PAGE_SIZE) + tail,
                            lambda b, p, pt: (layer, pt[b, n_pages - 1 - (p * npp + i)]) + zeros)

    pages, page_args = [], []
    for i in range(npp):
        pages += [page(i, HD, DIFF_DV), page(i, HD, DIFF_DV), page(i, HF, FOX_DH), page(i, HF, FOX_DH), page(i, HF)]
        page_args += [cache_dk, cache_dv, cache_fk, cache_fv, cache_lf]
    ae = _suffix_matrix(HF)
    out_d = pl.BlockSpec((None, HD, DIFF_DV), lambda b, p, pt: (b, 0, 0))
    out_f = pl.BlockSpec((None, HF, FOX_DH), lambda b, p, pt: (b, 0, 0))
    grid_spec = pltpu.PrefetchScalarGridSpec(
        num_scalar_prefetch=1,
        grid=(DB, steps),
        in_specs=[
            pl.BlockSpec((None, 3, HD, DIFF_DV), lambda b, p, pt: (b, 0, 0, 0)),
            pl.BlockSpec((None, 3, HF, FOX_DH), lambda b, p, pt: (b, 0, 0, 0)),
            pl.BlockSpec((None, HF, 1), lambda b, p, pt: (b, 0, 0)),
            *pages,
            pl.BlockSpec(ae.shape, lambda b, p, pt: (0, 0)),
            pl.BlockSpec((1, DIFF_DV), lambda b, p, pt: (0, 0)),
            pl.BlockSpec((4, DIFF_DC), lambda b, p, pt: (0, 0)),
        ],
        out_specs=[out_d, out_f],
        scratch_shapes=[
            pltpu.VMEM((2 * HD, DIFF_DV), BF16), pltpu.VMEM((2 * HD, 1), F32), pltpu.VMEM((2 * HD, 1), F32),
            pltpu.VMEM((2 * HD, DIFF_DV), F32),
            pltpu.VMEM((HF, FOX_DH), BF16), pltpu.VMEM((HF, 1), F32), pltpu.VMEM((HF, 1), F32),
            pltpu.VMEM((HF, FOX_DH), F32),
            pltpu.VMEM((HF, 1), F32),
        ],
    )
    return pl.pallas_call(
        functools.partial(_decode_kernel, lam_init=lam_init, npp=npp),
        grid_spec=grid_spec,
        out_shape=[jax.ShapeDtypeStruct((DB, HD, DIFF_DV), F32), jax.ShapeDtypeStruct((DB, HF, FOX_DH), F32)],
        compiler_params=_cparams("parallel", "arbitrary"),
        name="decode_attention",
    )(page_table, tok_d, tok_f, slogf_t, *page_args, ae, dsub_g.reshape(1, DIFF_DV), jnp.stack(lam_vecs))


def _merge_kernel(od_ref, of_ref, wa_ref, wb_ref, ga_ref, gb_ref, *refs):
    if len(refs) == 1:
        (o_ref,) = refs
    else:
        u_ref, v_ref, o_ref, u16_ref, v16_ref = refs
        u16_ref[...] = u_ref[...].astype(u16_ref.dtype)
        v16_ref[...] = v_ref[...].astype(v16_ref.dtype)
    ya = jnp.dot(od_ref[...], wa_ref[...], preferred_element_type=F32)
    yb = jnp.dot(of_ref[...], wb_ref[...], preferred_element_type=F32)
    o_ref[...] = (ga_ref[...].astype(F32) * ya + gb_ref[...].astype(F32) * yb).astype(o_ref.dtype)


def merge(od, of, w_a, w_b, sga, sgb, tables=None):
    T, Wd = od.shape
    Wf = of.shape[1]
    D = w_a.shape[1]
    tm = _tile(T, 1024 if tables is None else 512)
    tn = _tile(D, 512)
    nj = D // tn
    in_specs = [
        pl.BlockSpec((tm, Wd), lambda i, j: (i, 0)), pl.BlockSpec((tm, Wf), lambda i, j: (i, 0)),
        pl.BlockSpec((Wd, tn), lambda i, j: (0, j)), pl.BlockSpec((Wf, tn), lambda i, j: (0, j)),
        pl.BlockSpec((tm, tn), lambda i, j: (i, j)), pl.BlockSpec((tm, tn), lambda i, j: (i, j)),
    ]
    out_specs = [pl.BlockSpec((tm, tn), lambda i, j: (i, j))]
    out_shape = [jax.ShapeDtypeStruct((T, D), BF16)]
    args = [od, of, w_a, w_b, sga, sgb]
    if tables is not None:
        steps = (T // tm) * nj
        NE, DT = tables[0].shape
        rows = NE // steps
        assert rows * steps == NE and rows % 16 == 0
        tab = pl.BlockSpec((rows, DT), lambda i, j: (i * nj + j, 0))
        in_specs += [tab, tab]
        out_specs += [tab, tab]
        out_shape += [jax.ShapeDtypeStruct((NE, DT), BF16)] * 2
        args += list(tables)
    outs = pl.pallas_call(
        _merge_kernel,
        grid=(T // tm, nj),
        in_specs=in_specs,
        out_specs=out_specs,
        out_shape=out_shape,
        compiler_params=_cparams("parallel", "arbitrary"),
        name="merge",
    )(*args)
    return outs[0] if tables is None else outs


def _out_proj_kernel(mix_ref, w_ref, x_ref, o_ref):
    o_ref[...] = x_ref[...] + jnp.dot(mix_ref[...], w_ref[...], preferred_element_type=F32)


def out_proj_residual(mix, w_o, x):
    T, D = mix.shape
    N = w_o.shape[1]
    tm = _tile(T, 1024)
    tn = _tile(N, 512)
    return pl.pallas_call(
        _out_proj_kernel,
        grid=(T // tm, N // tn),
        in_specs=[
            pl.BlockSpec((tm, D), lambda i, j: (i, 0)), pl.BlockSpec((D, tn), lambda i, j: (0, j)),
            pl.BlockSpec((tm, tn), lambda i, j: (i, j)),
        ],
        out_specs=pl.BlockSpec((tm, tn), lambda i, j: (i, j)),
        out_shape=jax.ShapeDtypeStruct((T, N), F32),
        compiler_params=_cparams("parallel", "arbitrary"),
        name="out_proj_residual",
    )(mix, w_o, x)


def _peer_query_kernel(x_ref, g_ref, wq_ref, keys_ref, hb_ref, st_ref, hb_scr):
    j = pl.program_id(1)

    @pl.when(j == 0)
    def _():
        x = x_ref[...]
        ms = jnp.mean(x * x, axis=-1, keepdims=True)
        hb = (x * lax.rsqrt(ms + EPS) * g_ref[...]).astype(BF16)
        hb_scr[...] = hb
        hb_ref[...] = hb

    q = jnp.dot(hb_scr[...], wq_ref[...], preferred_element_type=F32).astype(BF16)
    for g in range(q.shape[1] // PEER_DHALF):
        st_ref[g] = lax.dot_general(keys_ref[g], q[:, g * PEER_DHALF:(g + 1) * PEER_DHALF],
                                    (((1,), (1,)), ((), ())), preferred_element_type=F32)


def peer_query(x, g, wq, keys):
    T, D = x.shape
    NQ = wq.shape[1]
    tm = _tile(T, 512)
    gpt = 4
    tn = gpt * PEER_DHALF
    return pl.pallas_call(
        _peer_query_kernel,
        grid=(T // tm, NQ // tn),
        in_specs=[
            pl.BlockSpec((tm, D), lambda i, j: (i, 0)), pl.BlockSpec((1, D), lambda i, j: (0, 0)),
            pl.BlockSpec((D, tn), lambda i, j: (0, j)),
            pl.BlockSpec((gpt, PEER_NKEYS, PEER_DHALF), lambda i, j: (j, 0, 0)),
        ],
        out_specs=[pl.BlockSpec((tm, D), lambda i, j: (i, 0)),
                   pl.BlockSpec((gpt, PEER_NKEYS, tm), lambda i, j: (j, 0, i))],
        out_shape=[jax.ShapeDtypeStruct((T, D), BF16),
                   jax.ShapeDtypeStruct((NQ // PEER_DHALF, PEER_NKEYS, T), F32)],
        scratch_shapes=[pltpu.VMEM((tm, D), BF16)],
        compiler_params=_cparams("parallel", "arbitrary"),
        name="peer_query",
    )(x, g.reshape(1, D), wq, keys)


def _topk_rows(xs, n):
    R, tt = xs[0].shape
    rows = lax.broadcasted_iota(jnp.int32, (R, tt), 0).astype(F32)
    slot = lax.broadcasted_iota(jnp.int32, (n, tt), 0)

    def body(r, carry):
        out = []
        for x, rank, vals in carry:
            m = jnp.max(x, axis=0, keepdims=True)
            idx = jnp.min(jnp.where(x == m, rows, float(R)), axis=0, keepdims=True)
            hit = rows == idx
            out.append((jnp.where(hit, NEG_INF, x), jnp.where(hit, lax.convert_element_type(r, F32), rank),
                        jnp.where(slot == r, m, vals)))
        return tuple(out)

    init = tuple((x, jnp.full((R, tt), float(n), F32), jnp.zeros((n, tt), F32)) for x in xs)
    res = lax.fori_loop(0, n, body, init)
    return [(rank, vals) for _, rank, vals in res]


def _peer_topk_kernel(s_ref, n0_ref, e0_ref, r1_ref, e1_ref):
    K = PEER_TOPK
    KB = K // 2

    def head(h, carry):
        s0 = s_ref[2 * h]
        s1 = s_ref[2 * h + 1]
        (rank0, v0), (rank1, v1) = _topk_rows((s0, s1), K)
        cand = jnp.concatenate([v0[0:1, :] + v1] + [v0[a:a + 1, :] + v1[0:KB, :] for a in range(1, K)], axis=0)
        row = lax.broadcasted_iota(jnp.int32, cand.shape, 0)
        ra = jnp.where(row < K, 0, (row - K) // KB + 1)
        rb = jnp.where(row < K, row, (row - K) % KB)
        cand = jnp.where((ra + 1) * (rb + 1) <= K, cand, NEG_INF)
        ((crank, cvals),) = _topk_rows((cand,), K)
        sel = crank < float(K)
        z = jnp.sum(jnp.where(sel, jnp.exp(cand - cvals[0:1, :]), 0.0), axis=0, keepdims=True)
        n0 = jnp.zeros_like(s0)
        for a in range(K):
            lo, hi = (0, K) if a == 0 else (K + KB * (a - 1), K + KB * a)
            cnt = jnp.sum(jnp.where(sel[lo:hi, :], 1.0, 0.0), axis=0, keepdims=True)
            n0 = jnp.where(rank0 == float(a), cnt, n0)
        n0_ref[h] = n0
        e0_ref[h] = jnp.exp(s0 - v0[0:1, :])
        r1_ref[h] = rank1.astype(r1_ref.dtype)
        e1_ref[h] = (jnp.exp(s1 - v1[0:1, :]) / z).astype(e1_ref.dtype)
        return carry

    lax.fori_loop(0, s_ref.shape[0] // 2, head, 0)


def peer_topk(scores_t):
    G, NK, T = scores_t.shape
    tt = _tile(T, LANES)
    spec = pl.BlockSpec((G // 2, NK, tt), lambda i: (0, 0, i))
    return pl.pallas_call(
        _peer_topk_kernel,
        grid=(T // tt,),
        in_specs=[pl.BlockSpec((G, NK, tt), lambda i: (0, 0, i))],
        out_specs=[spec] * 4,
        out_shape=[jax.ShapeDtypeStruct((G // 2, NK, T), dt) for dt in (F32, F32, BF16, BF16)],
        compiler_params=_cparams("parallel"),
        name="peer_topk",
    )(scores_t)


def _peer_mix_kernel(hb_ref, u_ref, v_ref, n0_ref, e0_ref, r1_ref, e1_ref, x_ref, o_ref):
    e = pl.program_id(1)

    @pl.when(e == 0)
    def _():
        o_ref[...] = x_ref[...]

    NK = PEER_NKEYS
    ipc = u_ref.shape[0] // NK
    at = lax.dot_general(u_ref[...], hb_ref[...], (((1,), (1,)), ((), ())), preferred_element_type=F32)
    act = (0.5 * at * (1.0 + lax.erf(at * (2.0 ** -0.5)))).astype(BF16)
    zero = jnp.zeros((), r1_ref.dtype)
    parts = []
    for ii in range(ipc):
        i = e * ipc + ii
        gate = None
        for h in range(n0_ref.shape[0]):
            n0row = n0_ref[h, pl.ds(i, 1), :].astype(r1_ref.dtype)
            e0row = e0_ref[h, pl.ds(i, 1), :].astype(e1_ref.dtype)
            term = jnp.where(r1_ref[h] < n0row, e1_ref[h], zero) * e0row
            gate = term if gate is None else gate + term
        parts.append(gate * act[ii * NK:(ii + 1) * NK, :])
    wt = jnp.concatenate(parts, axis=0)
    o_ref[...] += lax.dot_general(wt, v_ref[...], (((0,), (0,)), ((), ())), preferred_element_type=F32)


def peer_mix(hb, u_tab, v_tab, n0, e0, r1, e1, x):
    T, D = hb.shape
    NE = u_tab.shape[0]
    heads, NK, _ = n0.shape
    tm = _tile(T, 512)
    ec = 4 * NK
    once = dict(pipeline_mode=pl.Buffered(1))
    tok_in = functools.partial(pl.BlockSpec, (tm, D), lambda i, e: (i, 0))
    tab = pl.BlockSpec((ec, D), lambda i, e: (e, 0))
    desc = pl.BlockSpec((heads, NK, tm), lambda i, e: (0, 0, i), **once)
    return pl.pallas_call(
        _peer_mix_kernel,
        grid=(T // tm, NE // ec),
        in_specs=[tok_in(**once), tab, tab, desc, desc, desc, desc, tok_in(**once)],
        out_specs=pl.BlockSpec((tm, D), lambda i, e: (i, 0)),
        out_shape=jax.ShapeDtypeStruct((T, D), F32),
        compiler_params=_cparams("parallel", "arbitrary"),
        name="peer_mix",
    )(hb, u_tab, v_tab, n0, e0, r1, e1, x)


def _rope_tables(pos, n_rows):
    half = DIFF_DC // 2
    inv = ROPE_THETA ** (-jnp.arange(half, dtype=F32) / half)
    ang = pos.astype(F32)[:, None] * inv[None, :]
    cos = jnp.cos(ang)
    sin = jnp.sin(ang)
    cos2 = jnp.concatenate([cos, cos], axis=-1)
    sin2 = jnp.concatenate([-sin, sin], axis=-1)
    if cos2.shape[0] != n_rows:
        cos2 = jnp.broadcast_to(cos2, (n_rows, DIFF_DC))
        sin2 = jnp.broadcast_to(sin2, (n_rows, DIFF_DC))
    return cos2, sin2


def _token_front(x, pos_tables, w, l_params):
    cos, sin = pos_tables
    h = rmsnorm_cast(x, l_params["norm1_g"])
    dscale = DIFF_DC ** -0.5
    fscale = FOX_DH ** -0.5
    (dq,) = project(h, w["dq"], out_dtypes=(BF16,), gain=l_params["dq_g"], cos=cos, sin=sin, scale=dscale)
    dk32, dk16 = project(h, w["dk"], out_dtypes=(F32, BF16), gain=l_params["dk_g"], cos=cos, sin=sin)
    dv32, dv16 = project(h, w["dv"], out_dtypes=(F32, BF16))
    (fq,) = project(h, w["fq"], out_dtypes=(BF16,), gain=l_params["fq_g"], scale=fscale)
    fk32, fk16 = project(h, w["fk"], out_dtypes=(F32, BF16), gain=l_params["fk_g"])
    fv32, fv16 = project(h, w["fv"], out_dtypes=(F32, BF16))
    (sga,) = project(h, w["ga"], out_dtypes=(BF16,), act="sigmoid")
    (sgb,) = project(h, w["gb"], out_dtypes=(BF16,), act="sigmoid")
    return h, dict(dq=dq, dk32=dk32, dk16=dk16, dv32=dv32, dv16=dv16, fq=fq, fk32=fk32, fk16=fk16,
                   fv32=fv32, fv16=fv16, sga=sga, sgb=sgb)


def _token_back(x, od, of, t, w, l_params, tables16=None):
    if tables16 is None:
        mix, u16, v16 = merge(od, of, w["w_a"], w["w_b"], t["sga"], t["sgb"], tables=(w["peer_u"], w["peer_v"]))
    else:
        mix = merge(od, of, w["w_a"], w["w_b"], t["sga"], t["sgb"])
        u16, v16 = tables16
    x2 = out_proj_residual(mix, w["w_o"], x)
    hb, scores_t = peer_query(x2, l_params["norm2_g"], w["peer_wq"], w["peer_keys"])
    n0, e0, r1, e1 = peer_topk(scores_t)
    return peer_mix(hb, u16, v16, n0, e0, r1, e1, x2), (u16, v16)


def kernel(x_prompt, x_sample, cache_diff_k, cache_diff_v, cache_fox_k, cache_fox_v, cache_fox_logf, page_table, norm1_g, w_in, b_f, dq_g, dk_g, fq_g, fk_g, lam_q1, lam_k1, lam_q2, lam_k2, dsub_g, w_a, w_b, w_o, norm2_g, peer_wq, peer_keys, peer_u, peer_v):
    B, S, D = x_prompt.shape
    DB, DS, _ = x_sample.shape
    assert DS == 1
    depth = w_in.shape[0]
    n_pages = page_table.shape[1]
    past = n_pages * PAGE_SIZE
    WD = DIFF_HEADS * DIFF_DV
    WF = FOX_HEADS * FOX_DH
    sizes = (WD, WD, WD, WF, WF, WF, FOX_HEADS, D, D)
    names = ("dq", "dk", "dv", "fq", "fk", "fv", "fl", "ga", "gb")
    offs = [0]
    for n in sizes:
        offs.append(offs[-1] + n)

    pos_p = _rope_tables(jnp.arange(S, dtype=jnp.int32), S)
    pos_s = _rope_tables(past + jnp.arange(DS, dtype=jnp.int32), DB)

    xp = x_prompt.reshape(B * S, D)
    xs = x_sample.reshape(DB * DS, D)
    outs = [[] for _ in range(10)]
    for l in range(depth):
        lam_init = 0.8 - 0.6 * math.exp(-0.3 * l)
        w = {n: w_in[l, :, offs[i]:offs[i + 1]].astype(BF16) for i, n in enumerate(names)}
        w.update(w_a=w_a[l].astype(BF16), w_b=w_b[l].astype(BF16), w_o=w_o[l].astype(BF16),
                 peer_wq=peer_wq[l].astype(BF16),
                 peer_keys=peer_keys[l].reshape(PEER_HEADS * 2, PEER_NKEYS, PEER_DHALF).astype(BF16),
                 peer_u=peer_u[l], peer_v=peer_v[l])
        lp = dict(norm1_g=norm1_g[l], dq_g=dq_g[l], dk_g=dk_g[l], fq_g=fq_g[l], fk_g=fk_g[l], norm2_g=norm2_g[l])
        lam_vecs = (lam_q1[l], lam_k1[l], lam_q2[l], lam_k2[l])

        h, t = _token_front(xp, pos_p, w, lp)
        logf, ct, ck = forget_gate(h, w["fl"], b_f[l], B, S)
        od = diff_attention(t["dq"], t["dk16"], t["dv16"], dsub_g[l], lam_vecs, lam_init, B, S)
        of = fox_attention(t["fq"], t["fk16"], t["fv16"], ct, ck, B, S)
        xp, tables16 = _token_back(xp, od, of, t, w, lp)
        outs[0].append(t["dk32"].reshape(B, S, DIFF_HEADS, DIFF_DV))
        outs[1].append(t["dv32"].reshape(B, S, DIFF_HEADS, DIFF_DV))
        outs[2].append(t["fk32"].reshape(B, S, FOX_HEADS, FOX_DH))
        outs[3].append(t["fv32"].reshape(B, S, FOX_HEADS, FOX_DH))
        outs[4].append(logf)

        hs, ts_ = _token_front(xs, pos_s, w, lp)
        slogf, _, _ = forget_gate(hs, w["fl"], b_f[l], 1, DB)
        slogf = slogf.reshape(DB, DS, FOX_HEADS)
        tok_d = jnp.stack([ts_[n].astype(F32).reshape(DB, DIFF_HEADS, DIFF_DV) for n in ("dq", "dk16", "dv16")], axis=1)
        tok_f = jnp.stack([ts_[n].astype(F32).reshape(DB, FOX_HEADS, FOX_DH) for n in ("fq", "fk16", "fv16")], axis=1)
        sod, sof = decode_attention(
            page_table, l, tok_d, tok_f, slogf.reshape(DB, FOX_HEADS, 1),
            cache_diff_k, cache_diff_v, cache_fox_k, cache_fox_v, cache_fox_logf,
            dsub_g[l], lam_vecs, lam_init)
        sod = sod.astype(BF16)
        sof = sof.astype(BF16)
        xs, _ = _token_back(xs, sod.reshape(DB, WD), sof.reshape(DB, WF), ts_, w, lp, tables16)
        outs[5].append(ts_["dk32"].reshape(DB, DS, DIFF_HEADS, DIFF_DV))
        outs[6].append(ts_["dv32"].reshape(DB, DS, DIFF_HEADS, DIFF_DV))
        outs[7].append(ts_["fk32"].reshape(DB, DS, FOX_HEADS, FOX_DH))
        outs[8].append(ts_["fv32"].reshape(DB, DS, FOX_HEADS, FOX_DH))
        outs[9].append(slogf)

    return (xp.reshape(B, S, D), xs.reshape(DB, DS, D)) + tuple(jnp.stack(o) for o in outs)
```
